```python
import math
import jax, jax.numpy as jnp
from jax import lax
import numpy as np

D_MODEL = 1024
BATCH = 1
SEQ = 16384
DEPTH = 4

HEAD_DIM = 64
N_HEADS_A = 8
DILATED_GROUPS = ((128, 1), (512, 4), (2048, 16))
N_HEADS_B = 8
N_KV_B = 2
WINDOW_B = 128
ROPE_THETA = 10000.0
WIDTH_A = N_HEADS_A * HEAD_DIM
WIDTH_B = N_HEADS_B * HEAD_DIM
WIDTH_KV_B = N_KV_B * HEAD_DIM
IN_SECTIONS = (WIDTH_A, WIDTH_A, WIDTH_A, WIDTH_B, WIDTH_KV_B, WIDTH_KV_B, D_MODEL, D_MODEL)
IN_COLS = sum(IN_SECTIONS)
D_FF_DENSE = 256 * math.ceil(8 * D_MODEL / 3 / 256)
N_EXPERTS = 8
TOP_K = 2
D_FF_EXPERT = 7 * D_MODEL // 2
MOE_BLOCK = 512
N_DENSE = (DEPTH + 1) // 2
N_MOE = DEPTH // 2
DEEPNORM_ALPHA = (2 * DEPTH) ** 0.25
DEEPNORM_BETA = (8 * DEPTH) ** -0.25
LN_EPS = 1e-5
NEG_INF = -1e30

kernel_name = 'hybrid_dilated_window_gqa_moe_deepnorm_encoder'


def layer_norm(x, g, b):
    xf = x.astype(jnp.float32)
    mu = jnp.mean(xf, axis=-1, keepdims=True)
    var = jnp.mean(jnp.square(xf - mu), axis=-1, keepdims=True)
    return ((xf - mu) * lax.rsqrt(var + LN_EPS) * g + b).astype(x.dtype)


def rotary_tables(seq):
    inv_freq = ROPE_THETA ** (-jnp.arange(0, HEAD_DIM, 2, dtype=jnp.float32) / HEAD_DIM)
    ang = jnp.arange(seq, dtype=jnp.float32)[:, None] * inv_freq[None, :]
    return jnp.cos(ang), jnp.sin(ang)


def apply_rotary(t, cos, sin):
    tf = t.astype(jnp.float32)
    t1, t2 = tf[..., :HEAD_DIM // 2], tf[..., HEAD_DIM // 2:]
    c, s = cos[None, :, None, :], sin[None, :, None, :]
    return jnp.concatenate([t1 * c - t2 * s, t2 * c + t1 * s], axis=-1).astype(t.dtype)


def banded_window_attention(q, k, v, half_w, sink=None):
    b, l, hq, dh = q.shape
    hkv = k.shape[2]
    g = hq // hkv
    blk = half_w
    nb = -(-l // blk)
    lp = nb * blk
    qb = jnp.pad(q, ((0, 0), (0, lp - l), (0, 0), (0, 0))).reshape(b, nb, blk, hkv, g, dh)
    kv_pad = ((0, 0), (blk, lp - l + blk), (0, 0), (0, 0))
    kp = jnp.pad(k, kv_pad).reshape(b, nb + 2, blk, hkv, dh)
    vp = jnp.pad(v, kv_pad).reshape(b, nb + 2, blk, hkv, dh)
    kw = jnp.concatenate([kp[:, :nb], kp[:, 1:nb + 1], kp[:, 2:]], axis=2)
    vw = jnp.concatenate([vp[:, :nb], vp[:, 1:nb + 1], vp[:, 2:]], axis=2)
    s = jnp.einsum('bnqhgd,bnkhd->bnhgqk', qb, kw, preferred_element_type=jnp.float32) * (dh ** -0.5)
    q_pos = jnp.arange(nb)[:, None] * blk + jnp.arange(blk)[None, :]
    k_pos = (jnp.arange(nb)[:, None] - 1) * blk + jnp.arange(3 * blk)[None, :]
    rel = k_pos[:, None, :] - q_pos[:, :, None]
    valid = (jnp.abs(rel) <= half_w) & (k_pos[:, None, :] >= 0) & (k_pos[:, None, :] < l)
    s = jnp.where(valid[None, :, None, None], s, NEG_INF)
    m = jnp.max(s, axis=-1, keepdims=True)
    if sink is not None:
        sink_l = sink.astype(jnp.float32).reshape(1, 1, hkv, g, 1, 1)
        m = jnp.maximum(m, sink_l)
    p = jnp.exp(s - m)
    denom = jnp.sum(p, axis=-1, keepdims=True)
    if sink is not None:
        denom = denom + jnp.exp(sink_l - m)
    o = jnp.einsum('bnhgqk,bnkhd->bnqhgd', p / denom, vw.astype(jnp.float32))
    o = o.reshape(b, lp, hq, dh)[:, :l].astype(q.dtype)
    lse = (m + jnp.log(denom))[..., 0].transpose(0, 1, 4, 2, 3).reshape(b, lp, hq)[:, :l]
    return o, lse


def dilated_attention(q, k, v, window, dilation):
    b, s, h, dh = q.shape
    r = dilation
    n = s // r

    def by_stride(t):
        return t.reshape(b, n, r, h, dh).transpose(0, 2, 1, 3, 4).reshape(b * r, n, h, dh)

    o, lse = banded_window_attention(by_stride(q), by_stride(k), by_stride(v), window // (2 * r))
    o = o.reshape(b, r, n, h, dh).transpose(0, 2, 1, 3, 4).reshape(b, s, h, dh)
    lse = lse.reshape(b, r, n, h).transpose(0, 2, 1, 3).reshape(b, s, h)
    return o, lse


def token_mixer(x, w_in, b_gate, sink, w_br_a, w_br_b, w_out, cos, sin):
    b, s, _ = x.shape
    proj = x @ w_in
    parts, off = [], 0
    for width in IN_SECTIONS:
        parts.append(proj[..., off:off + width])
        off += width
    q_a, k_a, v_a, q_b, k_b, v_b, g_a, g_b = parts
    q_a = apply_rotary(q_a.reshape(b, s, N_HEADS_A, HEAD_DIM), cos, sin)
    k_a = apply_rotary(k_a.reshape(b, s, N_HEADS_A, HEAD_DIM), cos, sin)
    v_a = v_a.reshape(b, s, N_HEADS_A, HEAD_DIM)
    q_b = apply_rotary(q_b.reshape(b, s, N_HEADS_B, HEAD_DIM), cos, sin)
    k_b = apply_rotary(k_b.reshape(b, s, N_KV_B, HEAD_DIM), cos, sin)
    v_b = v_b.reshape(b, s, N_KV_B, HEAD_DIM)

    outs, lses = [], []
    for window, dilation in DILATED_GROUPS:
        o, l = dilated_attention(q_a, k_a, v_a, window, dilation)
        outs.append(o)
        lses.append(l)
    wts = jax.nn.softmax(jnp.stack(lses, axis=0), axis=0)
    y_a = jnp.einsum('gbsh,gbshd->bshd', wts, jnp.stack(outs, axis=0).astype(jnp.float32))
    y_a = y_a.astype(x.dtype).reshape(b, s, WIDTH_A)

    y_b, _ = banded_window_attention(q_b, k_b, v_b, WINDOW_B, sink)
    y_b = y_b.reshape(b, s, WIDTH_B)

    gate_a = jax.nn.sigmoid(g_a + b_gate[:D_MODEL])
    gate_b = jax.nn.sigmoid(g_b + b_gate[D_MODEL:])
    merged = gate_a * (y_a @ w_br_a) + gate_b * (y_b @ w_br_b)
    return merged @ w_out


def dense_swiglu(x, w1, w3, w2):
    return (jax.nn.silu(x @ w1) * (x @ w3)) @ w2


def moe_swiglu(x, w_router, w1, w3, w2):
    b, s, d = x.shape
    t = b * s
    xt = x.reshape(t, d)
    logits = (xt @ w_router).astype(jnp.float32)
    top_logits, top_idx = lax.top_k(logits, TOP_K)
    gates = jax.nn.softmax(top_logits, axis=-1)
    e_flat = top_idx.reshape(-1)
    tok_flat = jnp.repeat(jnp.arange(t), TOP_K)
    g_flat = gates.reshape(-1)
    order = jnp.argsort(e_flat)
    e_sorted, tok_sorted, g_sorted = e_flat[order], tok_flat[order], g_flat[order]
    counts = jnp.bincount(e_flat, length=N_EXPERTS)
    starts = jnp.cumsum(counts) - counts
    padded = (counts + MOE_BLOCK - 1) // MOE_BLOCK * MOE_BLOCK
    pad_ends = jnp.cumsum(padded)
    pad_starts = pad_ends - padded
    dest = pad_starts[e_sorted] + jnp.arange(t * TOP_K) - starts[e_sorted]
    n_rows = -(-(t * TOP_K + N_EXPERTS * (MOE_BLOCK - 1)) // MOE_BLOCK) * MOE_BLOCK
    n_blocks = n_rows // MOE_BLOCK
    xbuf = jnp.zeros((n_rows, d), x.dtype).at[dest].set(xt[tok_sorted])
    block_start = jnp.arange(n_blocks) * MOE_BLOCK
    block_expert = jnp.minimum(jnp.sum(block_start[:, None] >= pad_ends[None, :], axis=1), N_EXPERTS - 1)

    def expert_block(args):
        xb, e = args
        return (jax.nn.silu(xb @ w1[e]) * (xb @ w3[e])) @ w2[e]

    ybuf = lax.map(expert_block, (xbuf.reshape(n_blocks, MOE_BLOCK, d), block_expert)).reshape(n_rows, d)
    y = jnp.zeros((t, d), jnp.float32).at[tok_sorted].add(ybuf[dest].astype(jnp.float32) * g_sorted[:, None])
    return y.astype(x.dtype).reshape(b, s, d)


def setup_inputs(seed: int = 0) -> dict:
    key = jax.random.key(seed)
    ks = jax.random.split(key, 18)
    f32 = jnp.float32

    def nrm(k, shape, scale):
        return jax.random.normal(k, shape, f32) * scale

    beta = DEEPNORM_BETA
    col_scale = jnp.concatenate([
        jnp.ones((2 * WIDTH_A,), f32), jnp.full((WIDTH_A,), beta, f32),
        jnp.ones((WIDTH_B + WIDTH_KV_B,), f32), jnp.full((WIDTH_KV_B,), beta, f32),
        jnp.ones((2 * D_MODEL,), f32)])
    return {
        'x': nrm(ks[0], (BATCH, SEQ, D_MODEL), 1.0),
        'w_in': nrm(ks[1], (DEPTH, D_MODEL, IN_COLS), D_MODEL ** -0.5) * col_scale,
        'b_gate': nrm(ks[2], (DEPTH, 2 * D_MODEL), 0.02),
        'sink_logits': nrm(ks[3], (DEPTH, N_HEADS_B), 0.5),
        'w_br_a': nrm(ks[4], (DEPTH, WIDTH_A, D_MODEL), WIDTH_A ** -0.5 * beta),
        'w_br_b': nrm(ks[5], (DEPTH, WIDTH_B, D_MODEL), WIDTH_B ** -0.5 * beta),
        'w_out': nrm(ks[6], (DEPTH, D_MODEL, D_MODEL), D_MODEL ** -0.5 * beta),
        'ln_mix_g': 1.0 + nrm(ks[7], (DEPTH, D_MODEL), 0.02),
        'ln_mix_b': nrm(ks[8], (DEPTH, D_MODEL), 0.02),
        'w1_dense': nrm(ks[9], (N_DENSE, D_MODEL, D_FF_DENSE), D_MODEL ** -0.5 * beta),
        'w3_dense': nrm(ks[10], (N_DENSE, D_MODEL, D_FF_DENSE), D_MODEL ** -0.5 * beta),
        'w2_dense': nrm(ks[11], (N_DENSE, D_FF_DENSE, D_MODEL), D_FF_DENSE ** -0.5 * beta),
        'w_router': nrm(ks[12], (N_MOE, D_MODEL, N_EXPERTS), D_MODEL ** -0.5),
        'w1_exp': nrm(ks[13], (N_MOE, N_EXPERTS, D_MODEL, D_FF_EXPERT), D_MODEL ** -0.5 * beta),
        'w3_exp': nrm(ks[14], (N_MOE, N_EXPERTS, D_MODEL, D_FF_EXPERT), D_MODEL ** -0.5 * beta),
        'w2_exp': nrm(ks[15], (N_MOE, N_EXPERTS, D_FF_EXPERT, D_MODEL), D_FF_EXPERT ** -0.5 * beta),
        'ln_ffn_g': 1.0 + nrm(ks[16], (DEPTH, D_MODEL), 0.02),
        'ln_ffn_b': nrm(ks[17], (DEPTH, D_MODEL), 0.02),
    }


def reference(x, w_in, b_gate, sink_logits, w_br_a, w_br_b, w_out, ln_mix_g, ln_mix_b,
              w1_dense, w3_dense, w2_dense, w_router, w1_exp, w3_exp, w2_exp,
              ln_ffn_g, ln_ffn_b):
    cos, sin = rotary_tables(x.shape[1])
    for i in range(DEPTH):
        mix = token_mixer(x, w_in[i], b_gate[i], sink_logits[i], w_br_a[i], w_br_b[i], w_out[i], cos, sin)
        x = layer_norm(DEEPNORM_ALPHA * x + mix, ln_mix_g[i], ln_mix_b[i])
        j = i // 2
        if i % 2 == 0:
            ffn = dense_swiglu(x, w1_dense[j], w3_dense[j], w2_dense[j])
        else:
            ffn = moe_swiglu(x, w_router[j], w1_exp[j], w3_exp[j], w2_exp[j])
        x = layer_norm(DEEPNORM_ALPHA * x + ffn, ln_ffn_g[i], ln_ffn_b[i])
    return x
```

```python
import functools
import math

import jax
import jax.numpy as jnp
from jax import lax
from jax.experimental import pallas as pl
from jax.experimental.pallas import tpu as pltpu

F32 = jnp.float32
BF16 = jnp.bfloat16

D_MODEL = 1024
DEPTH = 4
HEAD_DIM = 64
N_HEADS_A = 8
DILATIONS = (1, 4, 16)
HALF_W_A = 64
N_HEADS_B = 8
N_KV_B = 2
HALF_W_B = 128
ROPE_THETA = 10000.0
WIDTH_A = N_HEADS_A * HEAD_DIM
WIDTH_B = N_HEADS_B * HEAD_DIM
WIDTH_KV_B = N_KV_B * HEAD_DIM
QKV_A = 3 * WIDTH_A
IN_COLS = QKV_A + WIDTH_B + 2 * WIDTH_KV_B + 2 * D_MODEL
D_FF_DENSE = 2816
N_EXPERTS = 8
D_FF_EXPERT = 3584
ALPHA = (2 * DEPTH) ** 0.25
LN_EPS = 1e-5
NEG_INF = -1e30

LANES = 128
VMEM_LIMIT = 56 * 1024 * 1024

IN_TM, IN_TN = 1024, 256
ATT_QS = 128
MERGE_TM = 512
DENSE_TM, DENSE_FC = 512, 256
ROUTE_TM = 512
MOE_BM = 512
MOE_FC = 1792
MOE_SUB = 256
SCATTER_TM = 512
COMBINE_TM = 256


def _params(sem):
    return pltpu.CompilerParams(dimension_semantics=sem, vmem_limit_bytes=VMEM_LIMIT)


def _layer_norm(z, g, b):
    mu = jnp.mean(z, axis=-1, keepdims=True)
    zc = z - mu
    var = jnp.mean(zc * zc, axis=-1, keepdims=True)
    return zc * lax.rsqrt(var + LN_EPS) * g + b


def _rope(a, cos, sin):
    lane = lax.broadcasted_iota(jnp.int32, a.shape, 1)
    fwd = pltpu.roll(a, 96, axis=1)
    bwd = pltpu.roll(a, 32, axis=1)
    partner = jnp.where((lane % HEAD_DIM) < HEAD_DIM // 2, fwd, bwd)
    return a * cos + partner * sin


def _inproj_kernel(x_ref, w_ref, cos_ref, sin_ref, qkv1_ref, qkv4_ref, qkv16_ref, qb_ref, kvb_ref,
                   g_ref, rs_ref):
    j = pl.program_id(1)
    tm = x_ref.shape[0]
    acc = jnp.dot(x_ref[...], w_ref[...], preferred_element_type=F32)
    n_a = QKV_A // IN_TN
    n_rope_a = 2 * WIDTH_A // IN_TN
    qb0 = n_a
    kvb0 = qb0 + WIDTH_B // IN_TN
    g0 = kvb0 + 1

    halves = [slice(h * LANES, (h + 1) * LANES) for h in range(IN_TN // LANES)]

    def rope2(v):
        c, s = cos_ref[...], sin_ref[...]
        return jnp.concatenate([_rope(v[:, sl], c, s) for sl in halves], axis=1)

    @pl.when(j < n_rope_a)
    def _():
        for h, sl in enumerate(halves):
            rs_ref[h] = _rope(acc[:, sl], cos_ref[...], sin_ref[...])

    @pl.when((j >= n_rope_a) & (j < n_a))
    def _():
        for h, sl in enumerate(halves):
            rs_ref[h] = acc[:, sl]

    @pl.when(j < n_a)
    def _():
        for h, sl in enumerate(halves):
            qkv1_ref[:, sl] = rs_ref[h].astype(BF16)
            for r, ref in ((4, qkv4_ref), (16, qkv16_ref)):
                for c in range(r):
                    ref[c, :, sl] = rs_ref[h, pl.ds(c, tm // r, stride=r), :].astype(BF16)

    @pl.when((j >= qb0) & (j < kvb0))
    def _():
        qb_ref[...] = rope2(acc).astype(BF16)

    @pl.when(j == kvb0)
    def _():
        kvb_ref[:, :LANES] = _rope(acc[:, :LANES], cos_ref[...], sin_ref[...]).astype(BF16)
        kvb_ref[:, LANES:] = acc[:, LANES:].astype(BF16)

    @pl.when(j >= g0)
    def _():
        g_ref[...] = acc.astype(BF16)


def _inproj(xb, w_in_b, layer, cos_t, sin_t):
    s = xb.shape[0]
    tm, tn = IN_TM, IN_TN
    n_a = QKV_A // tn
    qb0 = n_a
    kvb0 = qb0 + WIDTH_B // tn
    g0 = kvb0 + 1
    n_g = 2 * D_MODEL // tn
    grid = (s // tm, IN_COLS // tn)
    clampi = lambda v, lo, hi: jnp.minimum(jnp.maximum(v, lo), hi)
    out_shape = (
        jax.ShapeDtypeStruct((s, QKV_A), BF16),
        jax.ShapeDtypeStruct((4, s // 4, QKV_A), BF16),
        jax.ShapeDtypeStruct((16, s // 16, QKV_A), BF16),
        jax.ShapeDtypeStruct((s, WIDTH_B), BF16),
        jax.ShapeDtypeStruct((s, 2 * WIDTH_KV_B), BF16),
        jax.ShapeDtypeStruct((s, 2 * D_MODEL), BF16),
    )
    ja = lambda j: clampi(j, 0, n_a - 1)
    out_specs = (
        pl.BlockSpec((tm, tn), lambda i, j: (i, ja(j))),
        pl.BlockSpec((4, tm // 4, tn), lambda i, j: (0, i, ja(j))),
        pl.BlockSpec((16, tm // 16, tn), lambda i, j: (0, i, ja(j))),
        pl.BlockSpec((tm, tn), lambda i, j: (i, clampi(j - qb0, 0, WIDTH_B // tn - 1))),
        pl.BlockSpec((tm, tn), lambda i, j: (i, 0)),
        pl.BlockSpec((tm, tn), lambda i, j: (i, clampi(j - g0, 0, n_g - 1))),
    )
    in_specs = [
        pl.BlockSpec((tm, D_MODEL), lambda i, j: (i, 0)),
        pl.BlockSpec((None, D_MODEL, tn), lambda i, j: (layer, 0, j)),
        pl.BlockSpec((tm, LANES), lambda i, j: (i, 0)),
        pl.BlockSpec((tm, LANES), lambda i, j: (i, 0)),
    ]
    return pl.pallas_call(
        _inproj_kernel, out_shape=out_shape, grid=grid, in_specs=in_specs, out_specs=out_specs,
        scratch_shapes=[pltpu.VMEM((tn // LANES, tm, LANES), F32)],
        compiler_params=_params(("arbitrary", "arbitrary")), name="inproj",
    )(xb, w_in_b, cos_t, sin_t)


def _band_softmax(q, k, v, valid, sink=None):
    s = lax.dot_general(q, k, (((1,), (1,)), ((), ())), preferred_element_type=F32)
    s = jnp.where(valid, s, NEG_INF)
    m = jnp.max(s, axis=-1, keepdims=True)
    if sink is not None:
        m = jnp.maximum(m, sink)
    p = jnp.exp(s - m)
    d = jnp.sum(p, axis=-1, keepdims=True)
    if sink is not None:
        d = d + jnp.exp(sink - m)
    o = jnp.dot(p.astype(BF16), v, preferred_element_type=F32) / d
    return o, m + jnp.log(d)


def _window(r0, n, half_w, qs):
    kw = qs + 2 * half_w
    ks = pl.multiple_of(jnp.clip(r0 - half_w, 0, n - kw), 64)
    qpos = r0 + lax.broadcasted_iota(jnp.int32, (qs, kw), 0)
    kpos = ks + lax.broadcasted_iota(jnp.int32, (qs, kw), 1)
    return ks, kw, jnp.abs(kpos - qpos) <= half_w


def _attn_a_kernel(q_ref, k_ref, v_ref, o_ref, l_ref):
    i = pl.program_id(2)
    bq = q_ref.shape[0]
    n = k_ref.shape[0]
    qs = ATT_QS

    def body(sb, carry):
        off = pl.multiple_of(sb * qs, qs)
        ks, kw, valid = _window(i * bq + off, n, HALF_W_A, qs)
        q = q_ref[pl.ds(off, qs), :] * jnp.asarray(HEAD_DIM ** -0.5, BF16)
        kwin = k_ref[pl.ds(ks, kw), :]
        vwin = v_ref[pl.ds(ks, kw), :]
        outs, lses = [], []
        for hh in range(LANES // HEAD_DIM):
            sl = slice(hh * HEAD_DIM, (hh + 1) * HEAD_DIM)
            o, lse = _band_softmax(q[:, sl], kwin[:, sl], vwin[:, sl], valid)
            outs.append(o)
            lses.append(jnp.broadcast_to(lse, (qs, HEAD_DIM)))
        o_ref[pl.ds(off, qs), :] = jnp.concatenate(outs, axis=1).astype(BF16)
        l_ref[pl.ds(off, qs), :] = jnp.concatenate(lses, axis=1)
        return carry

    lax.fori_loop(0, bq // qs, body, 0)


def _attn_a(qkv_r):
    r, n, _ = qkv_r.shape
    bq = min(n, 1024)
    n_pair = WIDTH_A // LANES
    grid = (r, n_pair, n // bq)
    in_specs = [
        pl.BlockSpec((None, bq, LANES), lambda c, j, i: (c, i, j)),
        pl.BlockSpec((None, n, LANES), lambda c, j, i: (c, 0, n_pair + j)),
        pl.BlockSpec((None, n, LANES), lambda c, j, i: (c, 0, 2 * n_pair + j)),
    ]
    out_specs = (
        pl.BlockSpec((None, bq, LANES), lambda c, j, i: (c, i, j)),
        pl.BlockSpec((None, bq, LANES), lambda c, j, i: (c, i, j)),
    )
    out_shape = (jax.ShapeDtypeStruct((r, n, WIDTH_A), BF16), jax.ShapeDtypeStruct((r, n, WIDTH_A), F32))
    return pl.pallas_call(
        _attn_a_kernel, out_shape=out_shape, grid=grid, in_specs=in_specs, out_specs=out_specs,
        compiler_params=_params(("arbitrary",) * 3), name=f"attn_a_r{r}",
    )(qkv_r, qkv_r, qkv_r)


def _attn_b_kernel(sink_ref, q_ref, kv_ref, o_ref):
    i = pl.program_id(0)
    bq = q_ref.shape[0]
    n = kv_ref.shape[0]
    qs = ATT_QS
    group = N_HEADS_B // N_KV_B

    def body(sb, carry):
        off = pl.multiple_of(sb * qs, qs)
        ks, kw, valid = _window(i * bq + off, n, HALF_W_B, qs)
        q = q_ref[pl.ds(off, qs), :] * jnp.asarray(HEAD_DIM ** -0.5, BF16)
        kvwin = kv_ref[pl.ds(ks, kw), :]
        outs = []
        for h in range(N_HEADS_B):
            g = h // group
            kh = kvwin[:, g * HEAD_DIM:(g + 1) * HEAD_DIM]
            vh = kvwin[:, WIDTH_KV_B + g * HEAD_DIM:WIDTH_KV_B + (g + 1) * HEAD_DIM]
            o, _ = _band_softmax(q[:, h * HEAD_DIM:(h + 1) * HEAD_DIM], kh, vh, valid, sink=sink_ref[h])
            outs.append(o)
        o_ref[pl.ds(off, qs), :] = jnp.concatenate(outs, axis=1).astype(BF16)
        return carry

    lax.fori_loop(0, bq // qs, body, 0)


def _attn_b(qb, kvb, sink):
    s = qb.shape[0]
    bq = min(s, 1024)
    grid_spec = pltpu.PrefetchScalarGridSpec(
        num_scalar_prefetch=1, grid=(s // bq,),
        in_specs=[pl.BlockSpec((bq, WIDTH_B), lambda i, sk: (i, 0)),
                  pl.BlockSpec((s, 2 * WIDTH_KV_B), lambda i, sk: (0, 0))],
        out_specs=pl.BlockSpec((bq, WIDTH_B), lambda i, sk: (i, 0)),
    )
    return pl.pallas_call(
        _attn_b_kernel, out_shape=jax.ShapeDtypeStruct((s, WIDTH_B), BF16), grid_spec=grid_spec,
        compiler_params=_params(("arbitrary",)), name="attn_b",
    )(sink, qb, kvb)


def _merge_kernel(o1_ref, l1_ref, o4_ref, l4_ref, o16_ref, l16_ref, yb_ref, g_ref, x_ref, bg_ref,
                  wa_ref, wb_ref, wo_ref, lg_ref, lb_ref, x1_ref, x1b_ref, so4, sl4, so16, sl16):
    tm = x_ref.shape[0]
    for r, src_o, src_l, dst_o, dst_l in ((4, o4_ref, l4_ref, so4, sl4), (16, o16_ref, l16_ref, so16, sl16)):
        for c in range(r):
            for h in range(WIDTH_A // LANES):
                sl = slice(h * LANES, (h + 1) * LANES)
                dst_o[h, pl.ds(c, tm // r, stride=r), :] = src_o[c, :, sl].astype(F32)
                dst_l[h, pl.ds(c, tm // r, stride=r), :] = src_l[c, :, sl]
    slabs = lambda ref: jnp.concatenate([ref[h] for h in range(WIDTH_A // LANES)], axis=1)
    l1, l4, l16 = l1_ref[...], slabs(sl4), slabs(sl16)
    lmax = jnp.maximum(jnp.maximum(l1, l4), l16)
    e1, e4, e16 = jnp.exp(l1 - lmax), jnp.exp(l4 - lmax), jnp.exp(l16 - lmax)
    y_a = (e1 * o1_ref[...].astype(F32) + e4 * slabs(so4) + e16 * slabs(so16)) / (e1 + e4 + e16)
    ta = jnp.dot(y_a.astype(BF16), wa_ref[...], preferred_element_type=F32)
    tb = jnp.dot(yb_ref[...], wb_ref[...], preferred_element_type=F32)
    gates = jax.nn.sigmoid(g_ref[...].astype(F32) + bg_ref[...])
    merged = gates[:, :D_MODEL] * ta + gates[:, D_MODEL:] * tb
    mix = jnp.dot(merged.astype(BF16), wo_ref[...], preferred_element_type=F32)
    x1 = _layer_norm(ALPHA * x_ref[...] + mix, lg_ref[...], lb_ref[...])
    x1_ref[...] = x1
    x1b_ref[...] = x1.astype(BF16)


def _merge(o1, l1, o4, l4, o16, l16, yb, g, x, b_gate, w_br_a_b, w_br_b_b, w_out_b, ln_g, ln_b, layer):
    s = x.shape[0]
    tm = MERGE_TM
    row = lambda w: pl.BlockSpec((tm, w), lambda i: (i, 0))
    perm = lambda r: pl.BlockSpec((r, tm // r, WIDTH_A), lambda i: (0, i, 0))
    vec = lambda w: pl.BlockSpec((None, 1, w), lambda i: (layer, 0, 0))
    wsp = lambda k, n: pl.BlockSpec((None, k, n), lambda i: (layer, 0, 0))
    in_specs = [row(WIDTH_A), row(WIDTH_A), perm(4), perm(4), perm(16), perm(16), row(WIDTH_B),
                row(2 * D_MODEL), row(D_MODEL), vec(2 * D_MODEL),
                wsp(WIDTH_A, D_MODEL), wsp(WIDTH_B, D_MODEL), wsp(D_MODEL, D_MODEL),
                vec(D_MODEL), vec(D_MODEL)]
    out_shape = (jax.ShapeDtypeStruct((s, D_MODEL), F32), jax.ShapeDtypeStruct((s, D_MODEL), BF16))
    return pl.pallas_call(
        _merge_kernel, out_shape=out_shape, grid=(s // tm,), in_specs=in_specs,
        out_specs=(row(D_MODEL), row(D_MODEL)),
        scratch_shapes=[pltpu.VMEM((WIDTH_A // LANES, tm, LANES), F32)] * 4,
        compiler_params=_params(("arbitrary",)), name="merge_outproj_ln",
    )(o1, l1, o4, l4, o16, l16, yb, g, x, b_gate, w_br_a_b, w_br_b_b, w_out_b, ln_g, ln_b)


def _swiglu(xb, w1, w3):
    h = jnp.dot(xb, w1, preferred_element_type=F32)
    u = jnp.dot(xb, w3, preferred_element_type=F32)
    return (h * jax.nn.sigmoid(h) * u).astype(BF16)


def _dense_kernel(xb_ref, x_ref, w1_ref, w3_ref, w2_ref, lg_ref, lb_ref, x2_ref, x2b_ref, acc_ref):
    n_chunks = w1_ref.shape[0]
    acc_ref[...] = jnp.zeros_like(acc_ref)

    def body(c, carry):
        a = _swiglu(xb_ref[...], w1_ref[c], w3_ref[c])
        acc_ref[...] += jnp.dot(a, w2_ref[c], preferred_element_type=F32)
        return carry

    lax.fori_loop(0, n_chunks, body, 0)
    x2 = _layer_norm(ALPHA * x_ref[...] + acc_ref[...], lg_ref[...], lb_ref[...])
    x2_ref[...] = x2
    x2b_ref[...] = x2.astype(BF16)


def _dense_ffn(xb, x, w1d, w3d, w2d, ln_g, ln_b, layer, j):
    s = x.shape[0]
    tm, fc = DENSE_TM, DENSE_FC
    nc = D_FF_DENSE // fc
    row = lambda: pl.BlockSpec((tm, D_MODEL), lambda i: (i, 0))
    vec = lambda: pl.BlockSpec((None, 1, D_MODEL), lambda i: (layer, 0, 0))
    resident = dict(pipeline_mode=pl.Buffered(1))
    in_specs = [row(), row(),
                pl.BlockSpec((None, nc, D_MODEL, fc), lambda i: (j, 0, 0, 0), **resident),
                pl.BlockSpec((None, nc, D_MODEL, fc), lambda i: (j, 0, 0, 0), **resident),
                pl.BlockSpec((None, nc, fc, D_MODEL), lambda i: (j, 0, 0, 0), **resident),
                vec(), vec()]
    out_shape = (jax.ShapeDtypeStruct((s, D_MODEL), F32), jax.ShapeDtypeStruct((s, D_MODEL), BF16))
    return pl.pallas_call(
        _dense_kernel, out_shape=out_shape, grid=(s // tm,), in_specs=in_specs,
        out_specs=(row(), row()), scratch_shapes=[pltpu.VMEM((tm, D_MODEL), F32)],
        compiler_params=_params(("arbitrary",)), name="dense_ffn_ln",
    )(xb, x, w1d, w3d, w2d, ln_g, ln_b)


def _router_kernel(x_ref, wr_ref, idx_ref, gate_ref, rank_ref, cnt_ref, carry_ref):
    i = pl.program_id(0)
    tm = x_ref.shape[0]

    @pl.when(i == 0)
    def _():
        carry_ref[...] = jnp.zeros_like(carry_ref)

    logits = jnp.dot(x_ref[...], wr_ref[...], preferred_element_type=F32, precision=lax.Precision.HIGHEST)
    lt = logits.T[:N_EXPERTS, :]
    ii = lax.broadcasted_iota(jnp.int32, lt.shape, 0)
    m1 = jnp.max(lt, axis=0, keepdims=True)
    i1 = jnp.min(jnp.where(lt == m1, ii, N_EXPERTS), axis=0, keepdims=True)
    l2 = jnp.where(ii == i1, -jnp.inf, lt)
    m2 = jnp.max(l2, axis=0, keepdims=True)
    i2 = jnp.min(jnp.where(l2 == m2, ii, N_EXPERTS), axis=0, keepdims=True)
    e = jnp.exp(m2 - m1)
    gate_ref[0:1, :] = 1.0 / (1.0 + e)
    gate_ref[1:2, :] = e / (1.0 + e)
    idx_ref[0:1, :] = i1
    idx_ref[1:2, :] = i2
    member = (ii == i1) | (ii == i2)
    a = lax.broadcasted_iota(jnp.int32, (tm, tm), 0)
    b = lax.broadcasted_iota(jnp.int32, (tm, tm), 1)
    upper = jnp.where(a < b, 1.0, 0.0).astype(BF16)
    memf = jnp.where(member, 1.0, 0.0)
    rank = jnp.dot(memf.astype(BF16), upper, preferred_element_type=F32) + carry_ref[:, 0:1]
    rank_ref[0:1, :] = jnp.sum(jnp.where(ii == i1, rank, 0.0), axis=0, keepdims=True).astype(jnp.int32)
    rank_ref[1:2, :] = jnp.sum(jnp.where(ii == i2, rank, 0.0), axis=0, keepdims=True).astype(jnp.int32)
    carry_ref[...] = carry_ref[...] + jnp.sum(memf, axis=1, keepdims=True)
    cnt_ref[...] = carry_ref[...].astype(jnp.int32)


def _router(x, wr_pad):
    t = x.shape[0]
    tm = ROUTE_TM
    two = lambda: pl.BlockSpec((2, tm), lambda i: (0, i))
    out_shape = (jax.ShapeDtypeStruct((2, t), jnp.int32), jax.ShapeDtypeStruct((2, t), F32),
                 jax.ShapeDtypeStruct((2, t), jnp.int32), jax.ShapeDtypeStruct((N_EXPERTS, LANES), jnp.int32))
    return pl.pallas_call(
        _router_kernel, out_shape=out_shape, grid=(t // tm,),
        in_specs=[pl.BlockSpec((tm, D_MODEL), lambda i: (i, 0)),
                  pl.BlockSpec((D_MODEL, LANES), lambda i: (0, 0))],
        out_specs=(two(), two(), two(), pl.BlockSpec((N_EXPERTS, LANES), lambda i: (0, 0))),
        scratch_shapes=[pltpu.VMEM((N_EXPERTS, LANES), F32)],
        compiler_params=_params(("arbitrary",)), name="router",
    )(x, wr_pad)


def _row_copy(src_ref, src_row, dst_ref, dst_row, sem):
    return pltpu.make_async_copy(src_ref.at[pl.ds(src_row, 1)], dst_ref.at[pl.ds(dst_row, 1)], sem)


def _scatter_kernel(dest_ref, x_ref, zeros_ref, xbuf_ref, sem):
    del zeros_ref
    t = x_ref.shape[0]
    t0 = pl.program_id(0) * SCATTER_TM

    def start(r, carry):
        for k in range(2):
            _row_copy(x_ref, t0 + r, xbuf_ref, dest_ref[k * t + t0 + r], sem).start()
        return carry

    def wait(r, carry):
        for k in range(2):
            _row_copy(x_ref, 0, xbuf_ref, 0, sem).wait()
        return carry

    lax.fori_loop(0, SCATTER_TM, start, 0)
    lax.fori_loop(0, SCATTER_TM, wait, 0)


def _scatter_rows(dest, x, n_rows):
    t = x.shape[0]
    zeros = jnp.zeros((n_rows, D_MODEL), x.dtype)
    grid_spec = pltpu.PrefetchScalarGridSpec(
        num_scalar_prefetch=1, grid=(t // SCATTER_TM,),
        in_specs=[pl.BlockSpec(memory_space=pl.ANY), pl.BlockSpec(memory_space=pl.ANY)],
        out_specs=pl.BlockSpec(memory_space=pl.ANY),
        scratch_shapes=[pltpu.SemaphoreType.DMA(())],
    )
    return pl.pallas_call(
        _scatter_kernel, out_shape=jax.ShapeDtypeStruct((n_rows, D_MODEL), x.dtype), grid_spec=grid_spec,
        input_output_aliases={2: 0},
        compiler_params=pltpu.CompilerParams(dimension_semantics=("arbitrary",), has_side_effects=True),
        name="moe_scatter",
    )(dest, x, zeros)


def _expert_kernel(be_ref, nu_ref, x_ref, w1_ref, w3_ref, w2_ref, y_ref, xb_ref, acc_ref):
    i, c = pl.program_id(0), pl.program_id(1)

    @pl.when(i < nu_ref[0])
    def _():
        @pl.when(c == 0)
        def _():
            xb_ref[...] = x_ref[...].astype(BF16)
            acc_ref[...] = jnp.zeros_like(acc_ref)

        for k in range(MOE_FC // MOE_SUB):
            sl = slice(k * MOE_SUB, (k + 1) * MOE_SUB)
            a = _swiglu(xb_ref[...], w1_ref[:, sl], w3_ref[:, sl])
            acc_ref[...] += jnp.dot(a, w2_ref[sl, :], preferred_element_type=F32)

        @pl.when(c == pl.num_programs(1) - 1)
        def _():
            y_ref[...] = acc_ref[...]

    @pl.when(i >= nu_ref[0])
    def _():
        y_ref[...] = jnp.zeros_like(y_ref)


def _expert_ffn(block_expert, n_used, xbuf, w1e, w3e, w2e, j):
    n_rows = xbuf.shape[0]
    bm, fc = MOE_BM, MOE_FC
    nc = D_FF_EXPERT // fc
    row = lambda i, be, nu: jnp.minimum(i, nu[0] - 1)
    col = lambda i, c, nu: jnp.where(i < nu[0], c, nc - 1)
    grid_spec = pltpu.PrefetchScalarGridSpec(
        num_scalar_prefetch=2, grid=(n_rows // bm, nc),
        in_specs=[
            pl.BlockSpec((bm, D_MODEL), lambda i, c, be, nu: (row(i, be, nu), 0)),
            pl.BlockSpec((None, None, D_MODEL, fc), lambda i, c, be, nu: (j, be[row(i, be, nu)], 0, col(i, c, nu))),
            pl.BlockSpec((None, None, D_MODEL, fc), lambda i, c, be, nu: (j, be[row(i, be, nu)], 0, col(i, c, nu))),
            pl.BlockSpec((None, None, fc, D_MODEL), lambda i, c, be, nu: (j, be[row(i, be, nu)], col(i, c, nu), 0)),
        ],
        out_specs=pl.BlockSpec((bm, D_MODEL), lambda i, c, be, nu: (i, 0)),
        scratch_shapes=[pltpu.VMEM((bm, D_MODEL), BF16), pltpu.VMEM((bm, D_MODEL), F32)],
    )
    return pl.pallas_call(
        _expert_kernel, out_shape=jax.ShapeDtypeStruct((n_rows, D_MODEL), F32), grid_spec=grid_spec,
        compiler_params=_params(("arbitrary", "arbitrary")), name="expert_ffn",
    )(block_expert, n_used, xbuf, w1e, w3e, w2e)


def _combine_kernel(dest_ref, ybuf_ref, gate_ref, x_ref, lg_ref, lb_ref, x2_ref, x2b_ref, buf_ref, sem):
    tm = x_ref.shape[0]
    t = pl.num_programs(0) * tm
    t0 = pl.program_id(0) * tm

    def start(r, carry):
        for k in range(2):
            _row_copy(ybuf_ref, dest_ref[k * t + t0 + r], buf_ref.at[k], r, sem).start()
        return carry

    def wait(r, carry):
        for k in range(2):
            _row_copy(ybuf_ref, 0, buf_ref.at[k], 0, sem).wait()
        return carry

    lax.fori_loop(0, tm, start, 0)
    lax.fori_loop(0, tm, wait, 0)
    g = gate_ref[...]
    y = buf_ref[0] * g[:, 0:1] + buf_ref[1] * g[:, 1:2]
    x2 = _layer_norm(ALPHA * x_ref[...] + y, lg_ref[...], lb_ref[...])
    x2_ref[...] = x2
    x2b_ref[...] = x2.astype(BF16)


def _combine(dest, ybuf, gates_t, x, ln_g, ln_b, layer):
    t = x.shape[0]
    tm = COMBINE_TM
    row = lambda: pl.BlockSpec((tm, D_MODEL), lambda i, d: (i, 0))
    vec = lambda: pl.BlockSpec((None, 1, D_MODEL), lambda i, d: (layer, 0, 0))
    grid_spec = pltpu.PrefetchScalarGridSpec(
        num_scalar_prefetch=1, grid=(t // tm,),
        in_specs=[pl.BlockSpec(memory_space=pl.ANY), pl.BlockSpec((tm, 2), lambda i, d: (i, 0)), row(), vec(), vec()],
        out_specs=(row(), row()),
        scratch_shapes=[pltpu.VMEM((2, tm, D_MODEL), F32), pltpu.SemaphoreType.DMA(())],
    )
    out_shape = (jax.ShapeDtypeStruct((t, D_MODEL), F32), jax.ShapeDtypeStruct((t, D_MODEL), BF16))
    return pl.pallas_call(
        _combine_kernel, out_shape=out_shape, grid_spec=grid_spec,
        compiler_params=_params(("arbitrary",)), name="moe_combine_ln",
    )(dest, ybuf, gates_t, x, ln_g, ln_b)


def _moe_ffn(x, wr_pad, w1e, w3e, w2e, ln_g, ln_b, layer, j):
    t = x.shape[0]
    bm = MOE_BM
    idx, gates, rank, cnt = _router(x, wr_pad)
    counts = cnt[:, 0]
    padded = (counts + bm - 1) // bm * bm
    pad_ends = jnp.cumsum(padded)
    pad_starts = pad_ends - padded
    n_rows = -(-(2 * t + N_EXPERTS * (bm - 1)) // bm) * bm
    n_blocks = n_rows // bm
    dest = (pad_starts[idx] + rank).reshape(-1).astype(jnp.int32)
    block_start = jnp.arange(n_blocks, dtype=jnp.int32) * bm
    block_expert = jnp.minimum(jnp.sum(block_start[:, None] >= pad_ends[None, :], axis=1), N_EXPERTS - 1)
    n_used = (pad_ends[-1:] // bm).astype(jnp.int32)
    xbuf = _scatter_rows(dest, x, n_rows)
    ybuf = _expert_ffn(block_expert.astype(jnp.int32), n_used, xbuf, w1e, w3e, w2e, j)
    return _combine(dest, ybuf, gates.T, x, ln_g, ln_b, layer)


def _rotary_tables(seq):
    inv_freq = ROPE_THETA ** (-jnp.arange(0, HEAD_DIM, 2, dtype=F32) / HEAD_DIM)
    ang = jnp.arange(seq, dtype=F32)[:, None] * inv_freq[None, :]
    cos, sin = jnp.cos(ang), jnp.sin(ang)
    return jnp.tile(cos, (1, 4)), jnp.tile(jnp.concatenate([-sin, sin], axis=1), (1, 2))


def kernel(x, w_in, b_gate, sink_logits, w_br_a, w_br_b, w_out, ln_mix_g, ln_mix_b, w1_dense, w3_dense,
           w2_dense, w_router, w1_exp, w3_exp, w2_exp, ln_ffn_g, ln_ffn_b):
    b, s, d = x.shape
    assert b == 1 and d == D_MODEL and s % (16 * 2 * ATT_QS) == 0 and s % IN_TM == 0
    cos_t, sin_t = _rotary_tables(s)
    w_in_b = w_in.astype(BF16)
    w_br_a_b, w_br_b_b, w_out_b = w_br_a.astype(BF16), w_br_b.astype(BF16), w_out.astype(BF16)
    n_dense, nc = w1_dense.shape[0], D_FF_DENSE // DENSE_FC
    chunk_cols = lambda w: w.astype(BF16).reshape(n_dense, D_MODEL, nc, DENSE_FC).transpose(0, 2, 1, 3)
    w1d, w3d = chunk_cols(w1_dense), chunk_cols(w3_dense)
    w2d = w2_dense.astype(BF16).reshape(n_dense, nc, DENSE_FC, D_MODEL)
    w1e, w3e, w2e = w1_exp.astype(BF16), w3_exp.astype(BF16), w2_exp.astype(BF16)
    wr_pad = jnp.pad(w_router, ((0, 0), (0, 0), (0, LANES - N_EXPERTS)))
    vec3 = lambda v: v.reshape(v.shape[0], 1, v.shape[1])
    b_gate3, lmg, lmb, lfg, lfb = map(vec3, (b_gate, ln_mix_g, ln_mix_b, ln_ffn_g, ln_ffn_b))

    xf = x.reshape(s, d)
    xb = xf.astype(BF16)
    for i in range(DEPTH):
        qkv1, qkv4, qkv16, qb, kvb, g = _inproj(xb, w_in_b, i, cos_t, sin_t)
        o1, l1 = _attn_a(qkv1.reshape(1, s, QKV_A))
        o4, l4 = _attn_a(qkv4)
        o16, l16 = _attn_a(qkv16)
        yb = _attn_b(qb, kvb, sink_logits[i])
        xf, xb = _merge(o1.reshape(s, WIDTH_A), l1.reshape(s, WIDTH_A), o4, l4, o16, l16, yb, g, xf,
                        b_gate3, w_br_a_b, w_br_b_b, w_out_b, lmg, lmb, i)
        j = i // 2
        if i % 2 == 0:
            xf, xb = _dense_ffn(xb, xf, w1d, w3d, w2d, lfg, lfb, i, j)
        else:
            xf, xb = _moe_ffn(xf, wr_pad[j], w1e, w3e, w2e, lfg, lfb, i, j)
    return xf.reshape(b, s, d)
```

```python
import functools
import math

import jax
import jax.numpy as jnp
from jax import lax
from jax.experimental import pallas as pl
from jax.experimental.pallas import tpu as pltpu

F32 = jnp.float32
BF16 = jnp.bfloat16

D_MODEL = 1024
DEPTH = 4
HEAD_DIM = 64
N_HEADS_A = 8
DILATIONS = (1, 4, 16)
HALF_W_A = 64
N_HEADS_B = 8
N_KV_B = 2
HALF_W_B = 128
ROPE_THETA = 10000.0
WIDTH_A = N_HEADS_A * HEAD_DIM
WIDTH_B = N_HEADS_B * HEAD_DIM
WIDTH_KV_B = N_KV_B * HEAD_DIM
QKV_A = 3 * WIDTH_A
IN_COLS = QKV_A + WIDTH_B + 2 * WIDTH_KV_B + 2 * D_MODEL
D_FF_DENSE = 2816
N_EXPERTS = 8
D_FF_EXPERT = 3584
ALPHA = (2 * DEPTH) ** 0.25
LN_EPS = 1e-5
NEG_INF = -1e30

LANES = 128
VMEM_LIMIT = 56 * 1024 * 1024

IN_TM, IN_TN = 1024, 256
ATT_QS = 128
MERGE_TM = 512
DENSE_TM, DENSE_FC = 512, 256
ROUTE_TM = 512
MOE_BM = 512
MOE_FC = 1792
MOE_SUB = 256
SCATTER_TM = 512
COMBINE_TM = 256


def _params(sem):
    return pltpu.CompilerParams(dimension_semantics=sem, vmem_limit_bytes=VMEM_LIMIT)


def _layer_norm(z, g, b):
    mu = jnp.mean(z, axis=-1, keepdims=True)
    zc = z - mu
    var = jnp.mean(zc * zc, axis=-1, keepdims=True)
    return zc * lax.rsqrt(var + LN_EPS) * g + b


def _rope(a, cos, sin):
    lane = lax.broadcasted_iota(jnp.int32, a.shape, 1)
    fwd = pltpu.roll(a, 96, axis=1)
    bwd = pltpu.roll(a, 32, axis=1)
    partner = jnp.where((lane % HEAD_DIM) < HEAD_DIM // 2, fwd, bwd)
    return a * cos + partner * sin


def _inproj_kernel(x_ref, w_ref, rope_ref, qkv1_ref, qkv4_ref, qkv16_ref, qb_ref, kvb_ref, g_ref, rs_ref):
    j = pl.program_id(1)
    tm = x_ref.shape[0]
    acc = jnp.dot(x_ref[...], w_ref[...], preferred_element_type=F32)
    n_qa = WIDTH_A // IN_TN
    n_rope_a = 2 * WIDTH_A // IN_TN
    n_a = QKV_A // IN_TN
    qb0 = n_a
    kvb0 = qb0 + WIDTH_B // IN_TN
    g0 = kvb0 + 1
    halves = [slice(h * LANES, (h + 1) * LANES) for h in range(IN_TN // LANES)]
    rope_q = lambda v: _rope(v, rope_ref[:, 0:LANES], rope_ref[:, LANES:2 * LANES])
    rope_k = lambda v: _rope(v, rope_ref[:, 2 * LANES:3 * LANES], rope_ref[:, 3 * LANES:])

    @pl.when(j < n_qa)
    def _():
        for h, sl in enumerate(halves):
            rs_ref[h] = rope_q(acc[:, sl])

    @pl.when((j >= n_qa) & (j < n_rope_a))
    def _():
        for h, sl in enumerate(halves):
            rs_ref[h] = rope_k(acc[:, sl])

    @pl.when((j >= n_rope_a) & (j < n_a))
    def _():
        for h, sl in enumerate(halves):
            rs_ref[h] = acc[:, sl]

    @pl.when(j < n_a)
    def _():
        for h, sl in enumerate(halves):
            qkv1_ref[:, sl] = rs_ref[h].astype(BF16)
            for r, ref in ((4, qkv4_ref), (16, qkv16_ref)):
                for c in range(r):
                    ref[c, :, sl] = rs_ref[h, pl.ds(c, tm // r, stride=r), :].astype(BF16)

    @pl.when((j >= qb0) & (j < kvb0))
    def _():
        for sl in halves:
            qb_ref[:, sl] = rope_q(acc[:, sl]).astype(BF16)

    @pl.when(j == kvb0)
    def _():
        kvb_ref[:, :LANES] = rope_k(acc[:, :LANES]).astype(BF16)
        kvb_ref[:, LANES:] = acc[:, LANES:].astype(BF16)

    @pl.when(j >= g0)
    def _():
        g_ref[...] = acc.astype(BF16)


def _inproj(xb, w_in_b, layer, rope_t):
    s = xb.shape[0]
    tm, tn = IN_TM, IN_TN
    n_a = QKV_A // tn
    qb0 = n_a
    kvb0 = qb0 + WIDTH_B // tn
    g0 = kvb0 + 1
    n_g = 2 * D_MODEL // tn
    grid = (s // tm, IN_COLS // tn)
    clampi = lambda v, lo, hi: jnp.minimum(jnp.maximum(v, lo), hi)
    out_shape = (
        jax.ShapeDtypeStruct((s, QKV_A), BF16),
        jax.ShapeDtypeStruct((4, s // 4, QKV_A), BF16),
        jax.ShapeDtypeStruct((16, s // 16, QKV_A), BF16),
        jax.ShapeDtypeStruct((s, WIDTH_B), BF16),
        jax.ShapeDtypeStruct((s, 2 * WIDTH_KV_B), BF16),
        jax.ShapeDtypeStruct((s, 2 * D_MODEL), BF16),
    )
    ja = lambda j: clampi(j, 0, n_a - 1)
    out_specs = (
        pl.BlockSpec((tm, tn), lambda i, j: (i, ja(j))),
        pl.BlockSpec((4, tm // 4, tn), lambda i, j: (0, i, ja(j))),
        pl.BlockSpec((16, tm // 16, tn), lambda i, j: (0, i, ja(j))),
        pl.BlockSpec((tm, tn), lambda i, j: (i, clampi(j - qb0, 0, WIDTH_B // tn - 1))),
        pl.BlockSpec((tm, tn), lambda i, j: (i, 0)),
        pl.BlockSpec((tm, tn), lambda i, j: (i, clampi(j - g0, 0, n_g - 1))),
    )
    in_specs = [
        pl.BlockSpec((tm, D_MODEL), lambda i, j: (i, 0)),
        pl.BlockSpec((None, D_MODEL, tn), lambda i, j: (layer, 0, j)),
        pl.BlockSpec((tm, 4 * LANES), lambda i, j: (i, 0)),
    ]
    return pl.pallas_call(
        _inproj_kernel, out_shape=out_shape, grid=grid, in_specs=in_specs, out_specs=out_specs,
        scratch_shapes=[pltpu.VMEM((tn // LANES, tm, LANES), F32)],
        compiler_params=_params(("arbitrary", "arbitrary")), name="inproj",
    )(xb, w_in_b, rope_t)


def _pair_attention(q_pairs, k_pairs, v_pairs, valid, sinks=None):
    qs = q_pairs[0].shape[0]
    lane = lax.broadcasted_iota(jnp.int32, (qs, LANES), 1)
    lo = lane < HEAD_DIM
    nt = (((1,), (1,)), ((), ()))
    scores = []
    for qp, kp in zip(q_pairs, k_pairs):
        for half in (lo, ~lo):
            scores.append(lax.dot_general(jnp.where(half, qp, jnp.zeros_like(qp)), kp, nt,
                                          preferred_element_type=F32))
    stats = []
    for idx, s in enumerate(scores):
        s = jnp.where(valid, s, NEG_INF)
        m = jnp.max(s, axis=-1, keepdims=True)
        if sinks is not None:
            m = jnp.maximum(m, sinks[idx])
        p = jnp.exp2(s - m)
        d = jnp.sum(p, axis=-1, keepdims=True)
        if sinks is not None:
            d = d + jnp.exp2(sinks[idx] - m)
        stats.append((p.astype(BF16), m, d))
    outs, lses = [], []
    for j, vp in enumerate(v_pairs):
        (p0, m0, d0), (p1, m1, d1) = stats[2 * j], stats[2 * j + 1]
        o0 = jnp.dot(p0, vp, preferred_element_type=F32)
        o1 = jnp.dot(p1, vp, preferred_element_type=F32)
        d = jnp.where(lo, d0, d1)
        outs.append(jnp.where(lo, o0, o1) / d)
        lses.append(jnp.where(lo, m0, m1) + jnp.log2(d))
    return outs, lses


def _fill_slab(slab_ref, prev_ref, main_ref, next_ref):
    w, bq = prev_ref.shape[0], main_ref.shape[0]
    slab_ref[0:w] = prev_ref[...]
    slab_ref[w:w + bq] = main_ref[...]
    slab_ref[w + bq:] = next_ref[...]


def _band_mask(block_row0, off, n, half_w, qs):
    kw = qs + 2 * half_w
    row = lax.broadcasted_iota(jnp.int32, (qs, kw), 0)
    col = lax.broadcasted_iota(jnp.int32, (qs, kw), 1)
    kpos = block_row0 + off - half_w + col
    return (jnp.abs(col - half_w - row) <= half_w) & (kpos >= 0) & (kpos < n)


def _attn_a_kernel(q_ref, km_ref, kp_ref, kn_ref, vm_ref, vp_ref, vn_ref, o_ref, l_ref, ks_ref, vs_ref, *, n):
    i = pl.program_id(1)
    bq = q_ref.shape[0]
    qs, w = ATT_QS, HALF_W_A
    kw = qs + 2 * w
    n_pair = WIDTH_A // LANES
    _fill_slab(ks_ref, kp_ref, km_ref, kn_ref)
    _fill_slab(vs_ref, vp_ref, vm_ref, vn_ref)

    def body(sb, carry):
        off = pl.multiple_of(sb * qs, qs)
        valid = _band_mask(i * bq, off, n, w, qs)
        q = q_ref[pl.ds(off, qs), :]
        kwin = ks_ref[pl.ds(off, kw), :]
        vwin = vs_ref[pl.ds(off, kw), :]
        pairs = [slice(j * LANES, (j + 1) * LANES) for j in range(n_pair)]
        outs, lses = _pair_attention([q[:, sl] for sl in pairs], [kwin[:, sl] for sl in pairs],
                                     [vwin[:, sl] for sl in pairs], valid)
        for sl, o, lse in zip(pairs, outs, lses):
            o_ref[pl.ds(off, qs), sl] = o.astype(BF16)
            l_ref[pl.ds(off, qs), sl] = lse
        return carry

    lax.fori_loop(0, bq // qs, body, 0)


def _halo_specs(bq, w, n, width, col):
    nb = bq // w
    main = pl.BlockSpec((None, bq, width), lambda c, i: (c, i, col))
    prev = pl.BlockSpec((None, w, width), lambda c, i: (c, jnp.maximum(i * nb - 1, 0), col))
    nxt = pl.BlockSpec((None, w, width), lambda c, i: (c, jnp.minimum((i + 1) * nb, n // w - 1), col))
    return [main, prev, nxt]


def _attn_a(qkv_r):
    r, n, _ = qkv_r.shape
    bq = min(n, 1024)
    w = HALF_W_A
    blk = lambda: pl.BlockSpec((None, bq, WIDTH_A), lambda c, i: (c, i, 0))
    in_specs = [blk()] + _halo_specs(bq, w, n, WIDTH_A, 1) + _halo_specs(bq, w, n, WIDTH_A, 2)
    out_shape = (jax.ShapeDtypeStruct((r, n, WIDTH_A), BF16), jax.ShapeDtypeStruct((r, n, WIDTH_A), F32))
    return pl.pallas_call(
        functools.partial(_attn_a_kernel, n=n), out_shape=out_shape, grid=(r, n // bq),
        in_specs=in_specs, out_specs=(blk(), blk()),
        scratch_shapes=[pltpu.VMEM((bq + 2 * w, WIDTH_A), BF16)] * 2,
        compiler_params=_params(("arbitrary",) * 2), name=f"attn_a_r{r}",
    )(*([qkv_r] * 7))


def _attn_b_kernel(sink_ref, q_ref, kvm_ref, kvp_ref, kvn_ref, o_ref, slab_ref, *, n):
    i = pl.program_id(1)
    bq = q_ref.shape[0]
    qs, w = ATT_QS, HALF_W_B
    kw = qs + 2 * w
    n_pair = WIDTH_B // LANES
    _fill_slab(slab_ref, kvp_ref, kvm_ref, kvn_ref)
    log2e = math.log2(math.e)
    sinks = [sink_ref[hh * n_pair + j] * log2e for j in range(n_pair) for hh in range(2)]

    def body(sb, carry):
        off = pl.multiple_of(sb * qs, qs)
        valid = _band_mask(i * bq, off, n, w, qs)
        q = q_ref[pl.ds(off, qs), :]
        win = slab_ref[pl.ds(off, kw), :]
        kb, vb = win[:, :LANES], win[:, LANES:]
        pairs = [slice(j * LANES, (j + 1) * LANES) for j in range(n_pair)]
        outs, _ = _pair_attention([q[:, sl] for sl in pairs], [kb] * n_pair, [vb] * n_pair, valid, sinks)
        for sl, o in zip(pairs, outs):
            o_ref[pl.ds(off, qs), sl] = o.astype(BF16)
        return carry

    lax.fori_loop(0, bq // qs, body, 0)


def _attn_b(qb, kvb, sink):
    s = qb.shape[0]
    bq = min(s, 1024)
    w = HALF_W_B
    kv_w = 2 * WIDTH_KV_B
    drop = lambda spec: pl.BlockSpec(spec.block_shape, lambda c, i, sk, f=spec.index_map: f(c, i))
    grid_spec = pltpu.PrefetchScalarGridSpec(
        num_scalar_prefetch=1, grid=(1, s // bq),
        in_specs=[pl.BlockSpec((None, bq, WIDTH_B), lambda c, i, sk: (c, i, 0))]
        + [drop(sp) for sp in _halo_specs(bq, w, s, kv_w, 0)],
        out_specs=pl.BlockSpec((None, bq, WIDTH_B), lambda c, i, sk: (c, i, 0)),
        scratch_shapes=[pltpu.VMEM((bq + 2 * w, kv_w), BF16)],
    )
    kvb3 = kvb.reshape(1, s, kv_w)
    out = pl.pallas_call(
        functools.partial(_attn_b_kernel, n=s), out_shape=jax.ShapeDtypeStruct((1, s, WIDTH_B), BF16),
        grid_spec=grid_spec, compiler_params=_params(("arbitrary",) * 2), name="attn_b",
    )(sink, qb.reshape(1, s, WIDTH_B), kvb3, kvb3, kvb3)
    return out.reshape(s, WIDTH_B)


def _merge_kernel(o1_ref, l1_ref, o4_ref, l4_ref, o16_ref, l16_ref, yb_ref, g_ref, x_ref, bg_ref,
                  wa_ref, wb_ref, wo_ref, lg_ref, lb_ref, x1_ref, x1b_ref, so4, sl4, so16, sl16):
    tm = x_ref.shape[0]
    for r, src_o, src_l, dst_o, dst_l in ((4, o4_ref, l4_ref, so4, sl4), (16, o16_ref, l16_ref, so16, sl16)):
        for c in range(r):
            for h in range(WIDTH_A // LANES):
                sl = slice(h * LANES, (h + 1) * LANES)
                dst_o[h, pl.ds(c, tm // r, stride=r), :] = src_o[c, :, sl].astype(F32)
                dst_l[h, pl.ds(c, tm // r, stride=r), :] = src_l[c, :, sl]
    slabs = lambda ref: jnp.concatenate([ref[h] for h in range(WIDTH_A // LANES)], axis=1)
    l1, l4, l16 = l1_ref[...], slabs(sl4), slabs(sl16)
    lmax = jnp.maximum(jnp.maximum(l1, l4), l16)
    e1, e4, e16 = jnp.exp2(l1 - lmax), jnp.exp2(l4 - lmax), jnp.exp2(l16 - lmax)
    y_a = (e1 * o1_ref[...].astype(F32) + e4 * slabs(so4) + e16 * slabs(so16)) / (e1 + e4 + e16)
    ta = jnp.dot(y_a.astype(BF16), wa_ref[...], preferred_element_type=F32)
    tb = jnp.dot(yb_ref[...], wb_ref[...], preferred_element_type=F32)
    gates = jax.nn.sigmoid(g_ref[...].astype(F32) + bg_ref[...])
    merged = gates[:, :D_MODEL] * ta + gates[:, D_MODEL:] * tb
    mix = jnp.dot(merged.astype(BF16), wo_ref[...], preferred_element_type=F32)
    x1 = _layer_norm(ALPHA * x_ref[...] + mix, lg_ref[...], lb_ref[...])
    x1_ref[...] = x1
    x1b_ref[...] = x1.astype(BF16)


def _merge(o1, l1, o4, l4, o16, l16, yb, g, x, b_gate, w_br_a_b, w_br_b_b, w_out_b, ln_g, ln_b, layer):
    s = x.shape[0]
    tm = MERGE_TM
    row = lambda w: pl.BlockSpec((tm, w), lambda i: (i, 0))
    perm = lambda r: pl.BlockSpec((r, tm // r, WIDTH_A), lambda i: (0, i, 0))
    vec = lambda w: pl.BlockSpec((None, 1, w), lambda i: (layer, 0, 0))
    wsp = lambda k, n: pl.BlockSpec((None, k, n), lambda i: (layer, 0, 0))
    in_specs = [row(WIDTH_A), row(WIDTH_A), perm(4), perm(4), perm(16), perm(16), row(WIDTH_B),
                row(2 * D_MODEL), row(D_MODEL), vec(2 * D_MODEL),
                wsp(WIDTH_A, D_MODEL), wsp(WIDTH_B, D_MODEL), wsp(D_MODEL, D_MODEL),
                vec(D_MODEL), vec(D_MODEL)]
    out_shape = (jax.ShapeDtypeStruct((s, D_MODEL), F32), jax.ShapeDtypeStruct((s, D_MODEL), BF16))
    return pl.pallas_call(
        _merge_kernel, out_shape=out_shape, grid=(s // tm,), in_specs=in_specs,
        out_specs=(row(D_MODEL), row(D_MODEL)),
        scratch_shapes=[pltpu.VMEM((WIDTH_A // LANES, tm, LANES), F32)] * 4,
        compiler_params=_params(("arbitrary",)), name="merge_outproj_ln",
    )(o1, l1, o4, l4, o16, l16, yb, g, x, b_gate, w_br_a_b, w_br_b_b, w_out_b, ln_g, ln_b)


def _swiglu(xb, w1, w3):
    h = jnp.dot(xb, w1, preferred_element_type=F32)
    u = jnp.dot(xb, w3, preferred_element_type=F32)
    return (h * jax.nn.sigmoid(h) * u).astype(BF16)


def _dense_kernel(xb_ref, x_ref, w1_ref, w3_ref, w2_ref, lg_ref, lb_ref, x2_ref, x2b_ref, acc_ref):
    n_chunks = w1_ref.shape[0]
    acc_ref[...] = jnp.zeros_like(acc_ref)

    def body(c, carry):
        a = _swiglu(xb_ref[...], w1_ref[c], w3_ref[c])
        acc_ref[...] += jnp.dot(a, w2_ref[c], preferred_element_type=F32)
        return carry

    lax.fori_loop(0, n_chunks, body, 0)
    x2 = _layer_norm(ALPHA * x_ref[...] + acc_ref[...], lg_ref[...], lb_ref[...])
    x2_ref[...] = x2
    x2b_ref[...] = x2.astype(BF16)


def _dense_ffn(xb, x, w1d, w3d, w2d, ln_g, ln_b, layer, j):
    s = x.shape[0]
    tm, fc = DENSE_TM, DENSE_FC
    nc = D_FF_DENSE // fc
    row = lambda: pl.BlockSpec((tm, D_MODEL), lambda i: (i, 0))
    vec = lambda: pl.BlockSpec((None, 1, D_MODEL), lambda i: (layer, 0, 0))
    resident = dict(pipeline_mode=pl.Buffered(1))
    in_specs = [row(), row(),
                pl.BlockSpec((None, nc, D_MODEL, fc), lambda i: (j, 0, 0, 0), **resident),
                pl.BlockSpec((None, nc, D_MODEL, fc), lambda i: (j, 0, 0, 0), **resident),
                pl.BlockSpec((None, nc, fc, D_MODEL), lambda i: (j, 0, 0, 0), **resident),
                vec(), vec()]
    out_shape = (jax.ShapeDtypeStruct((s, D_MODEL), F32), jax.ShapeDtypeStruct((s, D_MODEL), BF16))
    return pl.pallas_call(
        _dense_kernel, out_shape=out_shape, grid=(s // tm,), in_specs=in_specs,
        out_specs=(row(), row()), scratch_shapes=[pltpu.VMEM((tm, D_MODEL), F32)],
        compiler_params=_params(("arbitrary",)), name="dense_ffn_ln",
    )(xb, x, w1d, w3d, w2d, ln_g, ln_b)


def _router_kernel(x_ref, wr_ref, idx_ref, gate_ref, rank_ref, cnt_ref, carry_ref):
    i = pl.program_id(0)
    tm = x_ref.shape[0]

    @pl.when(i == 0)
    def _():
        carry_ref[...] = jnp.zeros_like(carry_ref)

    logits = jnp.dot(x_ref[...], wr_ref[...], preferred_element_type=F32, precision=lax.Precision.HIGHEST)
    lt = logits.T[:N_EXPERTS, :]
    ii = lax.broadcasted_iota(jnp.int32, lt.shape, 0)
    m1 = jnp.max(lt, axis=0, keepdims=True)
    i1 = jnp.min(jnp.where(lt == m1, ii, N_EXPERTS), axis=0, keepdims=True)
    l2 = jnp.where(ii == i1, -jnp.inf, lt)
    m2 = jnp.max(l2, axis=0, keepdims=True)
    i2 = jnp.min(jnp.where(l2 == m2, ii, N_EXPERTS), axis=0, keepdims=True)
    e = jnp.exp(m2 - m1)
    gate_ref[0:1, :] = 1.0 / (1.0 + e)
    gate_ref[1:2, :] = e / (1.0 + e)
    idx_ref[0:1, :] = i1
    idx_ref[1:2, :] = i2
    member = (ii == i1) | (ii == i2)
    a = lax.broadcasted_iota(jnp.int32, (tm, tm), 0)
    b = lax.broadcasted_iota(jnp.int32, (tm, tm), 1)
    upper = jnp.where(a < b, 1.0, 0.0).astype(BF16)
    memf = jnp.where(member, 1.0, 0.0)
    rank = jnp.dot(memf.astype(BF16), upper, preferred_element_type=F32) + carry_ref[:, 0:1]
    rank_ref[0:1, :] = jnp.sum(jnp.where(ii == i1, rank, 0.0), axis=0, keepdims=True).astype(jnp.int32)
    rank_ref[1:2, :] = jnp.sum(jnp.where(ii == i2, rank, 0.0), axis=0, keepdims=True).astype(jnp.int32)
    carry_ref[...] = carry_ref[...] + jnp.sum(memf, axis=1, keepdims=True)
    cnt_ref[...] = carry_ref[...].astype(jnp.int32)


def _router(x, wr_pad):
    t = x.shape[0]
    tm = ROUTE_TM
    two = lambda: pl.BlockSpec((2, tm), lambda i: (0, i))
    out_shape = (jax.ShapeDtypeStruct((2, t), jnp.int32), jax.ShapeDtypeStruct((2, t), F32),
                 jax.ShapeDtypeStruct((2, t), jnp.int32), jax.ShapeDtypeStruct((N_EXPERTS, LANES), jnp.int32))
    return pl.pallas_call(
        _router_kernel, out_shape=out_shape, grid=(t // tm,),
        in_specs=[pl.BlockSpec((tm, D_MODEL), lambda i: (i, 0)),
                  pl.BlockSpec((D_MODEL, LANES), lambda i: (0, 0))],
        out_specs=(two(), two(), two(), pl.BlockSpec((N_EXPERTS, LANES), lambda i: (0, 0))),
        scratch_shapes=[pltpu.VMEM((N_EXPERTS, LANES), F32)],
        compiler_params=_params(("arbitrary",)), name="router",
    )(x, wr_pad)


def _row_copy(src_ref, src_row, dst_ref, dst_row, sem):
    return pltpu.make_async_copy(src_ref.at[pl.ds(src_row, 1)], dst_ref.at[pl.ds(dst_row, 1)], sem)


def _scatter_kernel(dest_ref, x_ref, zeros_ref, xbuf_ref, sem):
    del zeros_ref
    tm = x_ref.shape[0]
    t = pl.num_programs(0) * tm
    t0 = pl.program_id(0) * tm

    def start(r, carry):
        for k in range(2):
            _row_copy(x_ref, r, xbuf_ref, dest_ref[k * t + t0 + r], sem).start()
        return carry

    def wait(r, carry):
        for k in range(2):
            _row_copy(x_ref, 0, xbuf_ref, 0, sem).wait()
        return carry

    lax.fori_loop(0, tm, start, 0)
    lax.fori_loop(0, tm, wait, 0)


def _scatter_rows(dest, x, n_rows):
    t = x.shape[0]
    tm = SCATTER_TM
    zeros = jnp.zeros((n_rows, D_MODEL), x.dtype)
    grid_spec = pltpu.PrefetchScalarGridSpec(
        num_scalar_prefetch=1, grid=(t // tm,),
        in_specs=[pl.BlockSpec((tm, D_MODEL), lambda i, d: (i, 0)), pl.BlockSpec(memory_space=pl.ANY)],
        out_specs=pl.BlockSpec(memory_space=pl.ANY),
        scratch_shapes=[pltpu.SemaphoreType.DMA(())],
    )
    return pl.pallas_call(
        _scatter_kernel, out_shape=jax.ShapeDtypeStruct((n_rows, D_MODEL), x.dtype), grid_spec=grid_spec,
        input_output_aliases={2: 0},
        compiler_params=pltpu.CompilerParams(dimension_semantics=("arbitrary",), has_side_effects=True,
                                             disable_bounds_checks=True),
        name="moe_scatter",
    )(dest, x, zeros)


def _expert_kernel(be_ref, nu_ref, x_ref, w1_ref, w3_ref, w2_ref, y_ref, xb_ref, acc_ref):
    i, c = pl.program_id(0), pl.program_id(1)

    @pl.when(i < nu_ref[0])
    def _():
        @pl.when(c == 0)
        def _():
            xb_ref[...] = x_ref[...].astype(BF16)
            acc_ref[...] = jnp.zeros_like(acc_ref)

        for k in range(MOE_FC // MOE_SUB):
            sl = slice(k * MOE_SUB, (k + 1) * MOE_SUB)
            a = _swiglu(xb_ref[...], w1_ref[:, sl], w3_ref[:, sl])
            acc_ref[...] += jnp.dot(a, w2_ref[sl, :], preferred_element_type=F32)

        @pl.when(c == pl.num_programs(1) - 1)
        def _():
            y_ref[...] = acc_ref[...]

    @pl.when(i >= nu_ref[0])
    def _():
        y_ref[...] = jnp.zeros_like(y_ref)


def _expert_ffn(block_expert, n_used, xbuf, w1e, w3e, w2e, j):
    n_rows = xbuf.shape[0]
    bm, fc = MOE_BM, MOE_FC
    nc = D_FF_EXPERT // fc
    row = lambda i, be, nu: jnp.minimum(i, nu[0] - 1)
    col = lambda i, c, nu: jnp.where(i < nu[0], c, nc - 1)
    grid_spec = pltpu.PrefetchScalarGridSpec(
        num_scalar_prefetch=2, grid=(n_rows // bm, nc),
        in_specs=[
            pl.BlockSpec((bm, D_MODEL), lambda i, c, be, nu: (row(i, be, nu), 0)),
            pl.BlockSpec((None, None, D_MODEL, fc), lambda i, c, be, nu: (j, be[row(i, be, nu)], 0, col(i, c, nu))),
            pl.BlockSpec((None, None, D_MODEL, fc), lambda i, c, be, nu: (j, be[row(i, be, nu)], 0, col(i, c, nu))),
            pl.BlockSpec((None, None, fc, D_MODEL), lambda i, c, be, nu: (j, be[row(i, be, nu)], col(i, c, nu), 0)),
        ],
        out_specs=pl.BlockSpec((bm, D_MODEL), lambda i, c, be, nu: (i, 0)),
        scratch_shapes=[pltpu.VMEM((bm, D_MODEL), BF16), pltpu.VMEM((bm, D_MODEL), F32)],
    )
    return pl.pallas_call(
        _expert_kernel, out_shape=jax.ShapeDtypeStruct((n_rows, D_MODEL), F32), grid_spec=grid_spec,
        compiler_params=_params(("arbitrary", "arbitrary")), name="expert_ffn",
    )(block_expert, n_used, xbuf, w1e, w3e, w2e)


def _combine_kernel(dest_ref, ybuf_ref, gate_ref, x_ref, lg_ref, lb_ref, x2_ref, x2b_ref, buf_ref, sem):
    tm = x_ref.shape[0]
    t = pl.num_programs(0) * tm
    t0 = pl.program_id(0) * tm

    def start(r, carry):
        for k in range(2):
            _row_copy(ybuf_ref, dest_ref[k * t + t0 + r], buf_ref.at[k], r, sem).start()
        return carry

    def wait(r, carry):
        for k in range(2):
            _row_copy(ybuf_ref, 0, buf_ref.at[k], 0, sem).wait()
        return carry

    lax.fori_loop(0, tm, start, 0)
    lax.fori_loop(0, tm, wait, 0)
    g = gate_ref[...]
    y = buf_ref[0] * g[:, 0:1] + buf_ref[1] * g[:, 1:2]
    x2 = _layer_norm(ALPHA * x_ref[...] + y, lg_ref[...], lb_ref[...])
    x2_ref[...] = x2
    x2b_ref[...] = x2.astype(BF16)


def _combine(dest, ybuf, gates_t, x, ln_g, ln_b, layer):
    t = x.shape[0]
    tm = COMBINE_TM
    row = lambda: pl.BlockSpec((tm, D_MODEL), lambda i, d: (i, 0))
    vec = lambda: pl.BlockSpec((None, 1, D_MODEL), lambda i, d: (layer, 0, 0))
    grid_spec = pltpu.PrefetchScalarGridSpec(
        num_scalar_prefetch=1, grid=(t // tm,),
        in_specs=[pl.BlockSpec(memory_space=pl.ANY), pl.BlockSpec((tm, 2), lambda i, d: (i, 0)), row(), vec(), vec()],
        out_specs=(row(), row()),
        scratch_shapes=[pltpu.VMEM((2, tm, D_MODEL), F32), pltpu.SemaphoreType.DMA(())],
    )
    out_shape = (jax.ShapeDtypeStruct((t, D_MODEL), F32), jax.ShapeDtypeStruct((t, D_MODEL), BF16))
    return pl.pallas_call(
        _combine_kernel, out_shape=out_shape, grid_spec=grid_spec,
        compiler_params=pltpu.CompilerParams(dimension_semantics=("arbitrary",), vmem_limit_bytes=VMEM_LIMIT,
                                             disable_bounds_checks=True),
        name="moe_combine_ln",
    )(dest, ybuf, gates_t, x, ln_g, ln_b)


def _moe_ffn(x, wr_pad, w1e, w3e, w2e, ln_g, ln_b, layer, j):
    t = x.shape[0]
    bm = MOE_BM
    idx, gates, rank, cnt = _router(x, wr_pad)
    counts = cnt[:, 0]
    padded = (counts + bm - 1) // bm * bm
    pad_ends = jnp.cumsum(padded)
    pad_starts = pad_ends - padded
    n_rows = -(-(2 * t + N_EXPERTS * (bm - 1)) // bm) * bm
    n_blocks = n_rows // bm
    expert_ids = jnp.arange(N_EXPERTS, dtype=jnp.int32)[:, None, None]
    start_of = jnp.sum(jnp.where(idx[None] == expert_ids, pad_starts[:, None, None], 0), axis=0)
    dest = (start_of + rank).reshape(-1).astype(jnp.int32)
    block_start = jnp.arange(n_blocks, dtype=jnp.int32) * bm
    block_expert = jnp.minimum(jnp.sum(block_start[:, None] >= pad_ends[None, :], axis=1), N_EXPERTS - 1)
    n_used = (pad_ends[-1:] // bm).astype(jnp.int32)
    xbuf = _scatter_rows(dest, x, n_rows)
    ybuf = _expert_ffn(block_expert.astype(jnp.int32), n_used, xbuf, w1e, w3e, w2e, j)
    return _combine(dest, ybuf, gates.T, x, ln_g, ln_b, layer)


def _rotary_tables(seq):
    inv_freq = ROPE_THETA ** (-jnp.arange(0, HEAD_DIM, 2, dtype=F32) / HEAD_DIM)
    ang = jnp.arange(seq, dtype=F32)[:, None] * inv_freq[None, :]
    cos, sin = jnp.cos(ang), jnp.sin(ang)
    cos_t, sin_t = jnp.tile(cos, (1, 4)), jnp.tile(jnp.concatenate([-sin, sin], axis=1), (1, 2))
    q_scale = HEAD_DIM ** -0.5 * math.log2(math.e)
    return jnp.concatenate([cos_t * q_scale, sin_t * q_scale, cos_t, sin_t], axis=1)


_B_HEAD_ORDER = tuple(h for j in range(N_HEADS_B // N_KV_B) for h in (j, N_HEADS_B // N_KV_B + j))


def kernel(x, w_in, b_gate, sink_logits, w_br_a, w_br_b, w_out, ln_mix_g, ln_mix_b, w1_dense, w3_dense,
           w2_dense, w_router, w1_exp, w3_exp, w2_exp, ln_ffn_g, ln_ffn_b):
    b, s, d = x.shape
    assert b == 1 and d == D_MODEL and s % (16 * 2 * ATT_QS) == 0 and s % IN_TM == 0
    rope_t = _rotary_tables(s)
    order = jnp.asarray(_B_HEAD_ORDER)
    qb_cols = w_in[:, :, QKV_A:QKV_A + WIDTH_B].reshape(DEPTH, D_MODEL, N_HEADS_B, HEAD_DIM)[:, :, order]
    w_in_b = jnp.concatenate([w_in[:, :, :QKV_A], qb_cols.reshape(DEPTH, D_MODEL, WIDTH_B),
                              w_in[:, :, QKV_A + WIDTH_B:]], axis=2).astype(BF16)
    w_br_b_b = w_br_b.reshape(DEPTH, N_HEADS_B, HEAD_DIM, D_MODEL)[:, order].reshape(w_br_b.shape).astype(BF16)
    w_br_a_b, w_out_b = w_br_a.astype(BF16), w_out.astype(BF16)
    n_dense, nc = w1_dense.shape[0], D_FF_DENSE // DENSE_FC
    chunk_cols = lambda w: w.astype(BF16).reshape(n_dense, D_MODEL, nc, DENSE_FC).transpose(0, 2, 1, 3)
    w1d, w3d = chunk_cols(w1_dense), chunk_cols(w3_dense)
    w2d = w2_dense.astype(BF16).reshape(n_dense, nc, DENSE_FC, D_MODEL)
    w1e, w3e, w2e = w1_exp.astype(BF16), w3_exp.astype(BF16), w2_exp.astype(BF16)
    wr_pad = jnp.pad(w_router, ((0, 0), (0, 0), (0, LANES - N_EXPERTS)))
    vec3 = lambda v: v.reshape(v.shape[0], 1, v.shape[1])
    b_gate3, lmg, lmb, lfg, lfb = map(vec3, (b_gate, ln_mix_g, ln_mix_b, ln_ffn_g, ln_ffn_b))

    xf = x.reshape(s, d)
    xb = xf.astype(BF16)
    for i in range(DEPTH):
        qkv1, qkv4, qkv16, qb, kvb, g = _inproj(xb, w_in_b, i, rope_t)
        o1, l1 = _attn_a(qkv1.reshape(1, s, QKV_A))
        o4, l4 = _attn_a(qkv4)
        o16, l16 = _attn_a(qkv16)
        yb = _attn_b(qb, kvb, sink_logits[i])
        xf, xb = _merge(o1.reshape(s, WIDTH_A), l1.reshape(s, WIDTH_A), o4, l4, o16, l16, yb, g, xf,
                        b_gate3, w_br_a_b, w_br_b_b, w_out_b, lmg, lmb, i)
        j = i // 2
        if i % 2 == 0:
            xf, xb = _dense_ffn(xb, xf, w1d, w3d, w2d, lfg, lfb, i, j)
        else:
            xf, xb = _moe_ffn(xf, wr_pad[j], w1e, w3e, w2e, lfg, lfb, i, j)
    return xf.reshape(b, s, d)
```

```python
import functools
import math

import jax
import jax.numpy as jnp
from jax import lax
from jax.experimental import pallas as pl
from jax.experimental.pallas import tpu as pltpu

F32 = jnp.float32
BF16 = jnp.bfloat16

D_MODEL = 1024
DEPTH = 4
HEAD_DIM = 64
N_HEADS_A = 8
DILATIONS = (1, 4, 16)
HALF_W_A = 64
N_HEADS_B = 8
N_KV_B = 2
HALF_W_B = 128
ROPE_THETA = 10000.0
WIDTH_A = N_HEADS_A * HEAD_DIM
WIDTH_B = N_HEADS_B * HEAD_DIM
WIDTH_KV_B = N_KV_B * HEAD_DIM
QKV_A = 3 * WIDTH_A
IN_COLS = QKV_A + WIDTH_B + 2 * WIDTH_KV_B + 2 * D_MODEL
D_FF_DENSE = 2816
N_EXPERTS = 8
D_FF_EXPERT = 3584
ALPHA = (2 * DEPTH) ** 0.25
LN_EPS = 1e-5
NEG_INF = -1e30

LANES = 128
VMEM_LIMIT = 56 * 1024 * 1024

IN_TM, IN_TN = 512, 256
ATT_QS = 128
MERGE_TM = 512
DENSE_TM, DENSE_FC = 512, 256
ROUTE_TM = 512
MOE_BM = 512
MOE_FC = 1792
MOE_SUB = 256
SCATTER_TM = 512
COMBINE_TM = 256


def _params(sem):
    return pltpu.CompilerParams(dimension_semantics=sem, vmem_limit_bytes=VMEM_LIMIT)


def _layer_norm(z, g, b):
    mu = jnp.mean(z, axis=-1, keepdims=True)
    zc = z - mu
    var = jnp.mean(zc * zc, axis=-1, keepdims=True)
    return zc * lax.rsqrt(var + LN_EPS) * g + b


def _rope(a, cos, sin):
    return a * cos + pltpu.roll(a, LANES // 2, axis=1) * sin


def _inproj_kernel(x_ref, w_ref, rope_ref, qkv1_ref, qkv4_ref, qkv16_ref, qb_ref, kvb_ref, g_ref, rs_ref):
    tm = x_ref.shape[0]
    x = x_ref[...]
    rope_q = lambda v: _rope(v, rope_ref[:, 0:LANES], rope_ref[:, LANES:2 * LANES])
    rope_k = lambda v: _rope(v, rope_ref[:, 2 * LANES:3 * LANES], rope_ref[:, 3 * LANES:])
    plain = lambda v: v
    n_q, n_a = WIDTH_A // LANES, QKV_A // LANES
    qb0 = n_a
    kb0 = qb0 + WIDTH_B // LANES
    vb0 = kb0 + WIDTH_KV_B // LANES
    g0 = vb0 + WIDTH_KV_B // LANES
    per_dot = IN_TN // LANES

    for t in range(IN_COLS // IN_TN):
        acc = jnp.dot(x, w_ref[:, t * IN_TN:(t + 1) * IN_TN], preferred_element_type=F32)
        for h in range(per_dot):
            c = t * per_dot + h
            blk = acc[:, h * LANES:(h + 1) * LANES]
            if c < n_a:
                post = rope_q if c < n_q else rope_k if c < 2 * n_q else plain
                sl = slice(c * LANES, (c + 1) * LANES)
                rs_ref[c] = post(blk)
                qkv1_ref[:, sl] = rs_ref[c].astype(BF16)
                for r, ref in ((4, qkv4_ref), (16, qkv16_ref)):
                    for res in range(r):
                        ref[res, :, sl] = rs_ref[c, pl.ds(res, tm // r, stride=r), :].astype(BF16)
            elif c < kb0:
                qb_ref[:, (c - qb0) * LANES:(c - qb0 + 1) * LANES] = rope_q(blk).astype(BF16)
            elif c < vb0:
                kvb_ref[:, :LANES] = rope_k(blk).astype(BF16)
            elif c < g0:
                kvb_ref[:, LANES:] = blk.astype(BF16)
            else:
                g_ref[:, (c - g0) * LANES:(c - g0 + 1) * LANES] = blk.astype(BF16)


def _inproj(xb, w_in_b, layer, rope_t):
    s = xb.shape[0]
    tm = IN_TM
    out_shape = (
        jax.ShapeDtypeStruct((s, QKV_A), BF16),
        jax.ShapeDtypeStruct((4, s // 4, QKV_A), BF16),
        jax.ShapeDtypeStruct((16, s // 16, QKV_A), BF16),
        jax.ShapeDtypeStruct((s, WIDTH_B), BF16),
        jax.ShapeDtypeStruct((s, 2 * WIDTH_KV_B), BF16),
        jax.ShapeDtypeStruct((s, 2 * D_MODEL), BF16),
    )
    row = lambda w: pl.BlockSpec((tm, w), lambda i: (i, 0))
    out_specs = (
        row(QKV_A),
        pl.BlockSpec((4, tm // 4, QKV_A), lambda i: (0, i, 0)),
        pl.BlockSpec((16, tm // 16, QKV_A), lambda i: (0, i, 0)),
        row(WIDTH_B), row(2 * WIDTH_KV_B), row(2 * D_MODEL),
    )
    in_specs = [
        row(D_MODEL),
        pl.BlockSpec((None, D_MODEL, IN_COLS), lambda i: (layer, 0, 0), pipeline_mode=pl.Buffered(1)),
        row(4 * LANES),
    ]
    return pl.pallas_call(
        _inproj_kernel, out_shape=out_shape, grid=(s // tm,), in_specs=in_specs, out_specs=out_specs,
        scratch_shapes=[pltpu.VMEM((QKV_A // LANES, tm, LANES), F32)],
        compiler_params=_params(("arbitrary",)), name="inproj",
    )(xb, w_in_b, rope_t)


def _pair_attention(q_pairs, k_pairs, v_pairs, valid, sinks=None):
    qs = q_pairs[0].shape[0]
    lane = lax.broadcasted_iota(jnp.int32, (qs, LANES), 1)
    lo = lane < HEAD_DIM
    q_first = (lane % HEAD_DIM) < HEAD_DIM // 2
    nt = (((1,), (1,)), ((), ()))
    scores = []
    for qp, kp in zip(q_pairs, k_pairs):
        for half in (q_first, ~q_first):
            scores.append(lax.dot_general(jnp.where(half, qp, jnp.zeros_like(qp)), kp, nt,
                                          preferred_element_type=F32))
    stats = []
    for idx, s in enumerate(scores):
        s = jnp.where(valid, s, NEG_INF)
        m = jnp.max(s, axis=-1, keepdims=True)
        if sinks is not None:
            m = jnp.maximum(m, sinks[idx])
        p = jnp.exp2(s - m)
        d = jnp.sum(p, axis=-1, keepdims=True)
        if sinks is not None:
            d = d + jnp.exp2(sinks[idx] - m)
        stats.append((p.astype(BF16), m, d))
    outs, lses = [], []
    for j, vp in enumerate(v_pairs):
        (p0, m0, d0), (p1, m1, d1) = stats[2 * j], stats[2 * j + 1]
        o0 = jnp.dot(p0, vp, preferred_element_type=F32)
        o1 = jnp.dot(p1, vp, preferred_element_type=F32)
        d = jnp.where(lo, d0, d1)
        outs.append(jnp.where(lo, o0, o1) / d)
        lses.append(jnp.where(lo, m0, m1) + jnp.log2(d))
    return outs, lses


def _fill_slab(slab_ref, prev_ref, main_ref, next_ref):
    w, bq = prev_ref.shape[0], main_ref.shape[0]
    slab_ref[0:w] = prev_ref[...]
    slab_ref[w:w + bq] = main_ref[...]
    slab_ref[w + bq:] = next_ref[...]


def _band_mask(block_row0, off, n, half_w, qs):
    kw = qs + 2 * half_w
    row = lax.broadcasted_iota(jnp.int32, (qs, kw), 0)
    col = lax.broadcasted_iota(jnp.int32, (qs, kw), 1)
    kpos = block_row0 + off - half_w + col
    return (jnp.abs(col - half_w - row) <= half_w) & (kpos >= 0) & (kpos < n)


def _attn_a_kernel(q_ref, km_ref, kp_ref, kn_ref, vm_ref, vp_ref, vn_ref, o_ref, l_ref, ks_ref, vs_ref, *, n):
    i = pl.program_id(1)
    bq = q_ref.shape[0]
    qs, w = ATT_QS, HALF_W_A
    kw = qs + 2 * w
    n_pair = WIDTH_A // LANES
    _fill_slab(ks_ref, kp_ref, km_ref, kn_ref)
    _fill_slab(vs_ref, vp_ref, vm_ref, vn_ref)

    def body(sb, carry):
        off = pl.multiple_of(sb * qs, qs)
        valid = _band_mask(i * bq, off, n, w, qs)
        q = q_ref[pl.ds(off, qs), :]
        kwin = ks_ref[pl.ds(off, kw), :]
        vwin = vs_ref[pl.ds(off, kw), :]
        pairs = [slice(j * LANES, (j + 1) * LANES) for j in range(n_pair)]
        outs, lses = _pair_attention([q[:, sl] for sl in pairs], [kwin[:, sl] for sl in pairs],
                                     [vwin[:, sl] for sl in pairs], valid)
        for sl, o, lse in zip(pairs, outs, lses):
            o_ref[pl.ds(off, qs), sl] = o.astype(BF16)
            l_ref[pl.ds(off, qs), sl] = lse
        return carry

    lax.fori_loop(0, bq // qs, body, 0)


def _halo_specs(bq, w, n, width, col):
    nb = bq // w
    main = pl.BlockSpec((None, bq, width), lambda c, i: (c, i, col))
    prev = pl.BlockSpec((None, w, width), lambda c, i: (c, jnp.maximum(i * nb - 1, 0), col))
    nxt = pl.BlockSpec((None, w, width), lambda c, i: (c, jnp.minimum((i + 1) * nb, n // w - 1), col))
    return [main, prev, nxt]


def _attn_a(qkv_r):
    r, n, _ = qkv_r.shape
    bq = min(n, 1024)
    w = HALF_W_A
    blk = lambda: pl.BlockSpec((None, bq, WIDTH_A), lambda c, i: (c, i, 0))
    in_specs = [blk()] + _halo_specs(bq, w, n, WIDTH_A, 1) + _halo_specs(bq, w, n, WIDTH_A, 2)
    out_shape = (jax.ShapeDtypeStruct((r, n, WIDTH_A), BF16), jax.ShapeDtypeStruct((r, n, WIDTH_A), F32))
    return pl.pallas_call(
        functools.partial(_attn_a_kernel, n=n), out_shape=out_shape, grid=(r, n // bq),
        in_specs=in_specs, out_specs=(blk(), blk()),
        scratch_shapes=[pltpu.VMEM((bq + 2 * w, WIDTH_A), BF16)] * 2,
        compiler_params=_params(("arbitrary",) * 2), name=f"attn_a_r{r}",
    )(*([qkv_r] * 7))


def _attn_b_kernel(sink_ref, q_ref, kvm_ref, kvp_ref, kvn_ref, o_ref, slab_ref, *, n):
    i = pl.program_id(1)
    bq = q_ref.shape[0]
    qs, w = ATT_QS, HALF_W_B
    kw = qs + 2 * w
    n_pair = WIDTH_B // LANES
    _fill_slab(slab_ref, kvp_ref, kvm_ref, kvn_ref)
    log2e = math.log2(math.e)
    sinks = [sink_ref[hh * n_pair + j] * log2e for j in range(n_pair) for hh in range(2)]

    def body(sb, carry):
        off = pl.multiple_of(sb * qs, qs)
        valid = _band_mask(i * bq, off, n, w, qs)
        q = q_ref[pl.ds(off, qs), :]
        win = slab_ref[pl.ds(off, kw), :]
        kb, vb = win[:, :LANES], win[:, LANES:]
        pairs = [slice(j * LANES, (j + 1) * LANES) for j in range(n_pair)]
        outs, _ = _pair_attention([q[:, sl] for sl in pairs], [kb] * n_pair, [vb] * n_pair, valid, sinks)
        for sl, o in zip(pairs, outs):
            o_ref[pl.ds(off, qs), sl] = o.astype(BF16)
        return carry

    lax.fori_loop(0, bq // qs, body, 0)


def _attn_b(qb, kvb, sink):
    s = qb.shape[0]
    bq = min(s, 1024)
    w = HALF_W_B
    kv_w = 2 * WIDTH_KV_B
    drop = lambda spec: pl.BlockSpec(spec.block_shape, lambda c, i, sk, f=spec.index_map: f(c, i))
    grid_spec = pltpu.PrefetchScalarGridSpec(
        num_scalar_prefetch=1, grid=(1, s // bq),
        in_specs=[pl.BlockSpec((None, bq, WIDTH_B), lambda c, i, sk: (c, i, 0))]
        + [drop(sp) for sp in _halo_specs(bq, w, s, kv_w, 0)],
        out_specs=pl.BlockSpec((None, bq, WIDTH_B), lambda c, i, sk: (c, i, 0)),
        scratch_shapes=[pltpu.VMEM((bq + 2 * w, kv_w), BF16)],
    )
    kvb3 = kvb.reshape(1, s, kv_w)
    out = pl.pallas_call(
        functools.partial(_attn_b_kernel, n=s), out_shape=jax.ShapeDtypeStruct((1, s, WIDTH_B), BF16),
        grid_spec=grid_spec, compiler_params=_params(("arbitrary",) * 2), name="attn_b",
    )(sink, qb.reshape(1, s, WIDTH_B), kvb3, kvb3, kvb3)
    return out.reshape(s, WIDTH_B)


def _merge_kernel(o1_ref, l1_ref, o4_ref, l4_ref, o16_ref, l16_ref, yb_ref, g_ref, x_ref, bg_ref,
                  wa_ref, wb_ref, wo_ref, lg_ref, lb_ref, x1_ref, x1b_ref, so4, sl4, so16, sl16):
    tm = x_ref.shape[0]
    for r, src_o, src_l, dst_o, dst_l in ((4, o4_ref, l4_ref, so4, sl4), (16, o16_ref, l16_ref, so16, sl16)):
        for c in range(r):
            for h in range(WIDTH_A // LANES):
                sl = slice(h * LANES, (h + 1) * LANES)
                dst_o[h, pl.ds(c, tm // r, stride=r), :] = src_o[c, :, sl].astype(F32)
                dst_l[h, pl.ds(c, tm // r, stride=r), :] = src_l[c, :, sl]
    slabs = lambda ref: jnp.concatenate([ref[h] for h in range(WIDTH_A // LANES)], axis=1)
    l1, l4, l16 = l1_ref[...], slabs(sl4), slabs(sl16)
    lmax = jnp.maximum(jnp.maximum(l1, l4), l16)
    e1, e4, e16 = jnp.exp2(l1 - lmax), jnp.exp2(l4 - lmax), jnp.exp2(l16 - lmax)
    y_a = (e1 * o1_ref[...].astype(F32) + e4 * slabs(so4) + e16 * slabs(so16)) / (e1 + e4 + e16)
    ta = jnp.dot(y_a.astype(BF16), wa_ref[...], preferred_element_type=F32)
    tb = jnp.dot(yb_ref[...], wb_ref[...], preferred_element_type=F32)
    gates = jax.nn.sigmoid(g_ref[...].astype(F32) + bg_ref[...])
    merged = gates[:, :D_MODEL] * ta + gates[:, D_MODEL:] * tb
    mix = jnp.dot(merged.astype(BF16), wo_ref[...], preferred_element_type=F32)
    x1 = _layer_norm(ALPHA * x_ref[...] + mix, lg_ref[...], lb_ref[...])
    x1_ref[...] = x1
    x1b_ref[...] = x1.astype(BF16)


def _merge(o1, l1, o4, l4, o16, l16, yb, g, x, b_gate, w_br_a_b, w_br_b_b, w_out_b, ln_g, ln_b, layer):
    s = x.shape[0]
    tm = MERGE_TM
    row = lambda w: pl.BlockSpec((tm, w), lambda i: (i, 0))
    perm = lambda r: pl.BlockSpec((r, tm // r, WIDTH_A), lambda i: (0, i, 0))
    vec = lambda w: pl.BlockSpec((None, 1, w), lambda i: (layer, 0, 0))
    wsp = lambda k, n: pl.BlockSpec((None, k, n), lambda i: (layer, 0, 0))
    in_specs = [row(WIDTH_A), row(WIDTH_A), perm(4), perm(4), perm(16), perm(16), row(WIDTH_B),
                row(2 * D_MODEL), row(D_MODEL), vec(2 * D_MODEL),
                wsp(WIDTH_A, D_MODEL), wsp(WIDTH_B, D_MODEL), wsp(D_MODEL, D_MODEL),
                vec(D_MODEL), vec(D_MODEL)]
    out_shape = (jax.ShapeDtypeStruct((s, D_MODEL), F32), jax.ShapeDtypeStruct((s, D_MODEL), BF16))
    return pl.pallas_call(
        _merge_kernel, out_shape=out_shape, grid=(s // tm,), in_specs=in_specs,
        out_specs=(row(D_MODEL), row(D_MODEL)),
        scratch_shapes=[pltpu.VMEM((WIDTH_A // LANES, tm, LANES), F32)] * 4,
        compiler_params=_params(("arbitrary",)), name="merge_outproj_ln",
    )(o1, l1, o4, l4, o16, l16, yb, g, x, b_gate, w_br_a_b, w_br_b_b, w_out_b, ln_g, ln_b)


def _swiglu(xb, w1, w3):
    h = jnp.dot(xb, w1, preferred_element_type=F32)
    u = jnp.dot(xb, w3, preferred_element_type=F32)
    return (h * jax.nn.sigmoid(h) * u).astype(BF16)


def _dense_kernel(xb_ref, x_ref, w1_ref, w3_ref, w2_ref, lg_ref, lb_ref, x2_ref, x2b_ref, acc_ref):
    n_chunks = w1_ref.shape[0]
    acc_ref[...] = jnp.zeros_like(acc_ref)

    def body(c, carry):
        a = _swiglu(xb_ref[...], w1_ref[c], w3_ref[c])
        acc_ref[...] += jnp.dot(a, w2_ref[c], preferred_element_type=F32)
        return carry

    lax.fori_loop(0, n_chunks, body, 0)
    x2 = _layer_norm(ALPHA * x_ref[...] + acc_ref[...], lg_ref[...], lb_ref[...])
    x2_ref[...] = x2
    x2b_ref[...] = x2.astype(BF16)


def _dense_ffn(xb, x, w1d, w3d, w2d, ln_g, ln_b, layer, j):
    s = x.shape[0]
    tm, fc = DENSE_TM, DENSE_FC
    nc = D_FF_DENSE // fc
    row = lambda: pl.BlockSpec((tm, D_MODEL), lambda i: (i, 0))
    vec = lambda: pl.BlockSpec((None, 1, D_MODEL), lambda i: (layer, 0, 0))
    resident = dict(pipeline_mode=pl.Buffered(1))
    in_specs = [row(), row(),
                pl.BlockSpec((None, nc, D_MODEL, fc), lambda i: (j, 0, 0, 0), **resident),
                pl.BlockSpec((None, nc, D_MODEL, fc), lambda i: (j, 0, 0, 0), **resident),
                pl.BlockSpec((None, nc, fc, D_MODEL), lambda i: (j, 0, 0, 0), **resident),
                vec(), vec()]
    out_shape = (jax.ShapeDtypeStruct((s, D_MODEL), F32), jax.ShapeDtypeStruct((s, D_MODEL), BF16))
    return pl.pallas_call(
        _dense_kernel, out_shape=out_shape, grid=(s // tm,), in_specs=in_specs,
        out_specs=(row(), row()), scratch_shapes=[pltpu.VMEM((tm, D_MODEL), F32)],
        compiler_params=_params(("arbitrary",)), name="dense_ffn_ln",
    )(xb, x, w1d, w3d, w2d, ln_g, ln_b)


def _router_kernel(x_ref, wr_ref, idx_ref, gate_ref, rank_ref, cnt_ref, carry_ref):
    i = pl.program_id(0)
    tm = x_ref.shape[0]

    @pl.when(i == 0)
    def _():
        carry_ref[...] = jnp.zeros_like(carry_ref)

    logits = jnp.dot(x_ref[...], wr_ref[...], preferred_element_type=F32, precision=lax.Precision.HIGHEST)
    lt = logits.T[:N_EXPERTS, :]
    ii = lax.broadcasted_iota(jnp.int32, lt.shape, 0)
    m1 = jnp.max(lt, axis=0, keepdims=True)
    i1 = jnp.min(jnp.where(lt == m1, ii, N_EXPERTS), axis=0, keepdims=True)
    l2 = jnp.where(ii == i1, -jnp.inf, lt)
    m2 = jnp.max(l2, axis=0, keepdims=True)
    i2 = jnp.min(jnp.where(l2 == m2, ii, N_EXPERTS), axis=0, keepdims=True)
    e = jnp.exp(m2 - m1)
    gate_ref[0:1, :] = 1.0 / (1.0 + e)
    gate_ref[1:2, :] = e / (1.0 + e)
    idx_ref[0:1, :] = i1
    idx_ref[1:2, :] = i2
    member = (ii == i1) | (ii == i2)
    a = lax.broadcasted_iota(jnp.int32, (tm, tm), 0)
    b = lax.broadcasted_iota(jnp.int32, (tm, tm), 1)
    upper = jnp.where(a < b, 1.0, 0.0).astype(BF16)
    memf = jnp.where(member, 1.0, 0.0)
    rank = jnp.dot(memf.astype(BF16), upper, preferred_element_type=F32) + carry_ref[:, 0:1]
    rank_ref[0:1, :] = jnp.sum(jnp.where(ii == i1, rank, 0.0), axis=0, keepdims=True).astype(jnp.int32)
    rank_ref[1:2, :] = jnp.sum(jnp.where(ii == i2, rank, 0.0), axis=0, keepdims=True).astype(jnp.int32)
    carry_ref[...] = carry_ref[...] + jnp.sum(memf, axis=1, keepdims=True)
    cnt_ref[...] = carry_ref[...].astype(jnp.int32)


def _router(x, wr_pad):
    t = x.shape[0]
    tm = ROUTE_TM
    two = lambda: pl.BlockSpec((2, tm), lambda i: (0, i))
    out_shape = (jax.ShapeDtypeStruct((2, t), jnp.int32), jax.ShapeDtypeStruct((2, t), F32),
                 jax.ShapeDtypeStruct((2, t), jnp.int32), jax.ShapeDtypeStruct((N_EXPERTS, LANES), jnp.int32))
    return pl.pallas_call(
        _router_kernel, out_shape=out_shape, grid=(t // tm,),
        in_specs=[pl.BlockSpec((tm, D_MODEL), lambda i: (i, 0)),
                  pl.BlockSpec((D_MODEL, LANES), lambda i: (0, 0))],
        out_specs=(two(), two(), two(), pl.BlockSpec((N_EXPERTS, LANES), lambda i: (0, 0))),
        scratch_shapes=[pltpu.VMEM((N_EXPERTS, LANES), F32)],
        compiler_params=_params(("arbitrary",)), name="router",
    )(x, wr_pad)


def _row_copy(src_ref, src_row, dst_ref, dst_row, sem):
    return pltpu.make_async_copy(src_ref.at[pl.ds(src_row, 1)], dst_ref.at[pl.ds(dst_row, 1)], sem)


def _scatter_kernel(dest_ref, x_ref, zeros_ref, xbuf_ref, sem):
    del zeros_ref
    tm = x_ref.shape[0]
    t = pl.num_programs(0) * tm
    t0 = pl.program_id(0) * tm

    def start(r, carry):
        for k in range(2):
            _row_copy(x_ref, r, xbuf_ref, dest_ref[k * t + t0 + r], sem).start(priority=k)
        return carry

    def wait(r, carry):
        for k in range(2):
            _row_copy(x_ref, 0, xbuf_ref, 0, sem).wait()
        return carry

    lax.fori_loop(0, tm, start, 0)
    lax.fori_loop(0, tm, wait, 0)


def _scatter_rows(dest, x, n_rows):
    t = x.shape[0]
    tm = SCATTER_TM
    zeros = jnp.zeros((n_rows, D_MODEL), x.dtype)
    grid_spec = pltpu.PrefetchScalarGridSpec(
        num_scalar_prefetch=1, grid=(t // tm,),
        in_specs=[pl.BlockSpec((tm, D_MODEL), lambda i, d: (i, 0)), pl.BlockSpec(memory_space=pl.ANY)],
        out_specs=pl.BlockSpec(memory_space=pl.ANY),
        scratch_shapes=[pltpu.SemaphoreType.DMA(())],
    )
    return pl.pallas_call(
        _scatter_kernel, out_shape=jax.ShapeDtypeStruct((n_rows, D_MODEL), x.dtype), grid_spec=grid_spec,
        input_output_aliases={2: 0},
        compiler_params=pltpu.CompilerParams(dimension_semantics=("arbitrary",), has_side_effects=True,
                                             disable_bounds_checks=True),
        name="moe_scatter",
    )(dest, x, zeros)


def _expert_kernel(be_ref, nu_ref, x_ref, w1_ref, w3_ref, w2_ref, y_ref, xb_ref, acc_ref):
    i, c = pl.program_id(0), pl.program_id(1)

    @pl.when(i < nu_ref[0])
    def _():
        @pl.when(c == 0)
        def _():
            xb_ref[...] = x_ref[...].astype(BF16)
            acc_ref[...] = jnp.zeros_like(acc_ref)

        for k in range(MOE_FC // MOE_SUB):
            sl = slice(k * MOE_SUB, (k + 1) * MOE_SUB)
            a = _swiglu(xb_ref[...], w1_ref[:, sl], w3_ref[:, sl])
            acc_ref[...] += jnp.dot(a, w2_ref[sl, :], preferred_element_type=F32)

        @pl.when(c == pl.num_programs(1) - 1)
        def _():
            y_ref[...] = acc_ref[...]

    @pl.when(i >= nu_ref[0])
    def _():
        y_ref[...] = jnp.zeros_like(y_ref)


def _expert_ffn(block_expert, n_used, xbuf, w1e, w3e, w2e, j):
    n_rows = xbuf.shape[0]
    bm, fc = MOE_BM, MOE_FC
    nc = D_FF_EXPERT // fc
    row = lambda i, be, nu: jnp.minimum(i, jnp.maximum(nu[0] - 1, 0))
    col = lambda i, c, nu: jnp.where(i < nu[0], c, nc - 1)
    grid_spec = pltpu.PrefetchScalarGridSpec(
        num_scalar_prefetch=2, grid=(n_rows // bm, nc),
        in_specs=[
            pl.BlockSpec((bm, D_MODEL), lambda i, c, be, nu: (row(i, be, nu), 0)),
            pl.BlockSpec((None, None, D_MODEL, fc), lambda i, c, be, nu: (j, be[row(i, be, nu)], 0, col(i, c, nu))),
            pl.BlockSpec((None, None, D_MODEL, fc), lambda i, c, be, nu: (j, be[row(i, be, nu)], 0, col(i, c, nu))),
            pl.BlockSpec((None, None, fc, D_MODEL), lambda i, c, be, nu: (j, be[row(i, be, nu)], col(i, c, nu), 0)),
        ],
        out_specs=pl.BlockSpec((bm, D_MODEL), lambda i, c, be, nu: (i, 0)),
        scratch_shapes=[pltpu.VMEM((bm, D_MODEL), BF16), pltpu.VMEM((bm, D_MODEL), F32)],
    )
    return pl.pallas_call(
        _expert_kernel, out_shape=jax.ShapeDtypeStruct((n_rows, D_MODEL), F32), grid_spec=grid_spec,
        compiler_params=_params(("arbitrary", "arbitrary")), name="expert_ffn",
    )(block_expert, n_used, xbuf, w1e, w3e, w2e)


def _combine_kernel(dest_ref, ybuf_ref, gate_ref, x_ref, lg_ref, lb_ref, x2_ref, x2b_ref, buf_ref, sem):
    tm = x_ref.shape[0]
    t = pl.num_programs(0) * tm
    t0 = pl.program_id(0) * tm

    def start(r, carry):
        for k in range(2):
            _row_copy(ybuf_ref, dest_ref[k * t + t0 + r], buf_ref.at[k], r, sem).start(priority=k)
        return carry

    def wait(r, carry):
        for k in range(2):
            _row_copy(ybuf_ref, 0, buf_ref.at[k], 0, sem).wait()
        return carry

    lax.fori_loop(0, tm, start, 0)
    lax.fori_loop(0, tm, wait, 0)
    g = gate_ref[...]
    y = buf_ref[0] * g[:, 0:1] + buf_ref[1] * g[:, 1:2]
    x2 = _layer_norm(ALPHA * x_ref[...] + y, lg_ref[...], lb_ref[...])
    x2_ref[...] = x2
    x2b_ref[...] = x2.astype(BF16)


def _combine(dest, ybuf, gates_t, x, ln_g, ln_b, layer):
    t = x.shape[0]
    tm = COMBINE_TM
    row = lambda: pl.BlockSpec((tm, D_MODEL), lambda i, d: (i, 0))
    vec = lambda: pl.BlockSpec((None, 1, D_MODEL), lambda i, d: (layer, 0, 0))
    grid_spec = pltpu.PrefetchScalarGridSpec(
        num_scalar_prefetch=1, grid=(t // tm,),
        in_specs=[pl.BlockSpec(memory_space=pl.ANY), pl.BlockSpec((tm, 2), lambda i, d: (i, 0)), row(), vec(), vec()],
        out_specs=(row(), row()),
        scratch_shapes=[pltpu.VMEM((2, tm, D_MODEL), F32), pltpu.SemaphoreType.DMA(())],
    )
    out_shape = (jax.ShapeDtypeStruct((t, D_MODEL), F32), jax.ShapeDtypeStruct((t, D_MODEL), BF16))
    return pl.pallas_call(
        _combine_kernel, out_shape=out_shape, grid_spec=grid_spec,
        compiler_params=pltpu.CompilerParams(dimension_semantics=("arbitrary",), vmem_limit_bytes=VMEM_LIMIT,
                                             disable_bounds_checks=True),
        name="moe_combine_ln",
    )(dest, ybuf, gates_t, x, ln_g, ln_b)


def _moe_ffn(x, wr_pad, w1e, w3e, w2e, ln_g, ln_b, layer, j):
    t = x.shape[0]
    bm = MOE_BM
    idx, gates, rank, cnt = _router(x, wr_pad)
    counts = cnt[:, 0]
    padded = (counts + bm - 1) // bm * bm
    pad_ends = jnp.cumsum(padded)
    pad_starts = pad_ends - padded
    n_rows = -(-(2 * t + N_EXPERTS * (bm - 1)) // bm) * bm
    n_blocks = n_rows // bm
    expert_ids = jnp.arange(N_EXPERTS, dtype=jnp.int32)[:, None, None]
    start_of = jnp.sum(jnp.where(idx[None] == expert_ids, pad_starts[:, None, None], 0), axis=0)
    dest = (start_of + rank).reshape(-1).astype(jnp.int32)
    block_start = jnp.arange(n_blocks, dtype=jnp.int32) * bm
    block_expert = jnp.minimum(jnp.sum(block_start[:, None] >= pad_ends[None, :], axis=1), N_EXPERTS - 1)
    n_used = (pad_ends[-1:] // bm).astype(jnp.int32)
    xbuf = _scatter_rows(dest, x, n_rows)
    ybuf = _expert_ffn(block_expert.astype(jnp.int32), n_used, xbuf, w1e, w3e, w2e, j)
    return _combine(dest, ybuf, gates.T, x, ln_g, ln_b, layer)


def _rotary_tables(seq):
    inv_freq = ROPE_THETA ** (-jnp.arange(0, HEAD_DIM, 2, dtype=F32) / HEAD_DIM)
    ang = jnp.arange(seq, dtype=F32)[:, None] * inv_freq[None, :]
    cos, sin = jnp.cos(ang), jnp.sin(ang)
    cos_t, sin_t = jnp.tile(cos, (1, 4)), jnp.concatenate([-sin, -sin, sin, sin], axis=1)
    q_scale = HEAD_DIM ** -0.5 * math.log2(math.e)
    return jnp.concatenate([cos_t * q_scale, sin_t * q_scale, cos_t, sin_t], axis=1)


_B_HEAD_ORDER = tuple(h for j in range(N_HEADS_B // N_KV_B) for h in (j, N_HEADS_B // N_KV_B + j))


def kernel(x, w_in, b_gate, sink_logits, w_br_a, w_br_b, w_out, ln_mix_g, ln_mix_b, w1_dense, w3_dense,
           w2_dense, w_router, w1_exp, w3_exp, w2_exp, ln_ffn_g, ln_ffn_b):
    b, s, d = x.shape
    assert b == 1 and d == D_MODEL and s % (16 * 2 * ATT_QS) == 0 and s % IN_TM == 0
    rope_t = _rotary_tables(s)
    order = jnp.asarray(_B_HEAD_ORDER)

    def rotary_layout(w, head_order=None):
        heads = w.shape[-1] // HEAD_DIM
        w = w.reshape(DEPTH, D_MODEL, heads, 2, HEAD_DIM // 2)
        if head_order is not None:
            w = w[:, :, head_order]
        w = w.reshape(DEPTH, D_MODEL, heads // 2, 2, 2, HEAD_DIM // 2).transpose(0, 1, 2, 4, 3, 5)
        return w.reshape(DEPTH, D_MODEL, heads * HEAD_DIM)

    kb0 = QKV_A + WIDTH_B
    w_in_b = jnp.concatenate([
        rotary_layout(w_in[:, :, :WIDTH_A]), rotary_layout(w_in[:, :, WIDTH_A:2 * WIDTH_A]),
        w_in[:, :, 2 * WIDTH_A:QKV_A], rotary_layout(w_in[:, :, QKV_A:kb0], order),
        rotary_layout(w_in[:, :, kb0:kb0 + WIDTH_KV_B]), w_in[:, :, kb0 + WIDTH_KV_B:]], axis=2).astype(BF16)
    w_br_b_b = w_br_b.reshape(DEPTH, N_HEADS_B, HEAD_DIM, D_MODEL)[:, order].reshape(w_br_b.shape).astype(BF16)
    w_br_a_b, w_out_b = w_br_a.astype(BF16), w_out.astype(BF16)
    n_dense, nc = w1_dense.shape[0], D_FF_DENSE // DENSE_FC
    chunk_cols = lambda w: w.astype(BF16).reshape(n_dense, D_MODEL, nc, DENSE_FC).transpose(0, 2, 1, 3)
    w1d, w3d = chunk_cols(w1_dense), chunk_cols(w3_dense)
    w2d = w2_dense.astype(BF16).reshape(n_dense, nc, DENSE_FC, D_MODEL)
    w1e, w3e, w2e = w1_exp.astype(BF16), w3_exp.astype(BF16), w2_exp.astype(BF16)
    wr_pad = jnp.pad(w_router, ((0, 0), (0, 0), (0, LANES - N_EXPERTS)))
    vec3 = lambda v: v.reshape(v.shape[0], 1, v.shape[1])
    b_gate3, lmg, lmb, lfg, lfb = map(vec3, (b_gate, ln_mix_g, ln_mix_b, ln_ffn_g, ln_ffn_b))

    xf = x.reshape(s, d)
    xb = xf.astype(BF16)
    for i in range(DEPTH):
        qkv1, qkv4, qkv16, qb, kvb, g = _inproj(xb, w_in_b, i, rope_t)
        o1, l1 = _attn_a(qkv1.reshape(1, s, QKV_A))
        o4, l4 = _attn_a(qkv4)
        o16, l16 = _attn_a(qkv16)
        yb = _attn_b(qb, kvb, sink_logits[i])
        xf, xb = _merge(o1.reshape(s, WIDTH_A), l1.reshape(s, WIDTH_A), o4, l4, o16, l16, yb, g, xf,
                        b_gate3, w_br_a_b, w_br_b_b, w_out_b, lmg, lmb, i)
        j = i // 2
        if i % 2 == 0:
            xf, xb = _dense_ffn(xb, xf, w1d, w3d, w2d, lfg, lfb, i, j)
        else:
            xf, xb = _moe_ffn(xf, wr_pad[j], w1e, w3e, w2e, lfg, lfb, i, j)
    return xf.reshape(b, s, d)
```

```python
import functools
import math

import jax
import jax.numpy as jnp
from jax import lax
from jax.experimental import pallas as pl
from jax.experimental.pallas import tpu as pltpu

F32 = jnp.float32
BF16 = jnp.bfloat16

D_MODEL = 1024
DEPTH = 4
HEAD_DIM = 64
N_HEADS_A = 8
DILATIONS = (1, 4, 16)
HALF_W_A = 64
N_HEADS_B = 8
N_KV_B = 2
HALF_W_B = 128
ROPE_THETA = 10000.0
WIDTH_A = N_HEADS_A * HEAD_DIM
WIDTH_B = N_HEADS_B * HEAD_DIM
WIDTH_KV_B = N_KV_B * HEAD_DIM
QKV_A = 3 * WIDTH_A
IN_COLS = QKV_A + WIDTH_B + 2 * WIDTH_KV_B + 2 * D_MODEL
D_FF_DENSE = 2816
N_EXPERTS = 8
D_FF_EXPERT = 3584
ALPHA = (2 * DEPTH) ** 0.25
LN_EPS = 1e-5
NEG_INF = -1e30

LANES = 128
VMEM_LIMIT = 56 * 1024 * 1024

IN_TM, IN_TN = 512, 256
ATT_QS = 128
MERGE_TM = 512
DENSE_TM, DENSE_FC = 512, 256
ROUTE_TM = 512
MOE_BM = 512
MOE_FC = 1792
MOE_SUB = 256
SCATTER_TM = 512
COMBINE_TM = 256
DMA_UNROLL = 8


def _params(sem):
    return pltpu.CompilerParams(dimension_semantics=sem, vmem_limit_bytes=VMEM_LIMIT)


def _layer_norm(z, g, b):
    mu = jnp.mean(z, axis=-1, keepdims=True)
    zc = z - mu
    var = jnp.mean(zc * zc, axis=-1, keepdims=True)
    return zc * lax.rsqrt(var + LN_EPS) * g + b


def _rope(a, cos, sin):
    return a * cos + pltpu.roll(a, LANES // 2, axis=1) * sin


def _inproj_kernel(x_ref, w_ref, rope_ref, qkv1_ref, qkv4_ref, qkv16_ref, qb_ref, kvb_ref, g_ref, rs_ref):
    tm = x_ref.shape[0]
    x = x_ref[...]
    rope_q = lambda v: _rope(v, rope_ref[:, 0:LANES], rope_ref[:, LANES:2 * LANES])
    rope_k = lambda v: _rope(v, rope_ref[:, 2 * LANES:3 * LANES], rope_ref[:, 3 * LANES:])
    plain = lambda v: v
    n_q, n_a = WIDTH_A // LANES, QKV_A // LANES
    qb0 = n_a
    kb0 = qb0 + WIDTH_B // LANES
    vb0 = kb0 + WIDTH_KV_B // LANES
    g0 = vb0 + WIDTH_KV_B // LANES
    per_dot = IN_TN // LANES

    for t in range(IN_COLS // IN_TN):
        acc = jnp.dot(x, w_ref[:, t * IN_TN:(t + 1) * IN_TN], preferred_element_type=F32)
        for h in range(per_dot):
            c = t * per_dot + h
            blk = acc[:, h * LANES:(h + 1) * LANES]
            if c < n_a:
                post = rope_q if c < n_q else rope_k if c < 2 * n_q else plain
                sl = slice(c * LANES, (c + 1) * LANES)
                rs_ref[c] = post(blk)
                qkv1_ref[:, sl] = rs_ref[c].astype(BF16)
                for r, ref in ((4, qkv4_ref), (16, qkv16_ref)):
                    for res in range(r):
                        ref[res, :, sl] = rs_ref[c, pl.ds(res, tm // r, stride=r), :].astype(BF16)
            elif c < kb0:
                qb_ref[:, (c - qb0) * LANES:(c - qb0 + 1) * LANES] = rope_q(blk).astype(BF16)
            elif c < vb0:
                kvb_ref[:, :LANES] = rope_k(blk).astype(BF16)
            elif c < g0:
                kvb_ref[:, LANES:] = blk.astype(BF16)
            else:
                g_ref[:, (c - g0) * LANES:(c - g0 + 1) * LANES] = blk.astype(BF16)


def _inproj(xb, w_in_b, layer, rope_t):
    s = xb.shape[0]
    tm = IN_TM
    out_shape = (
        jax.ShapeDtypeStruct((s, QKV_A), BF16),
        jax.ShapeDtypeStruct((4, s // 4, QKV_A), BF16),
        jax.ShapeDtypeStruct((16, s // 16, QKV_A), BF16),
        jax.ShapeDtypeStruct((s, WIDTH_B), BF16),
        jax.ShapeDtypeStruct((s, 2 * WIDTH_KV_B), BF16),
        jax.ShapeDtypeStruct((s, 2 * D_MODEL), BF16),
    )
    row = lambda w: pl.BlockSpec((tm, w), lambda i: (i, 0))
    out_specs = (
        row(QKV_A),
        pl.BlockSpec((4, tm // 4, QKV_A), lambda i: (0, i, 0)),
        pl.BlockSpec((16, tm // 16, QKV_A), lambda i: (0, i, 0)),
        row(WIDTH_B), row(2 * WIDTH_KV_B), row(2 * D_MODEL),
    )
    in_specs = [
        row(D_MODEL),
        pl.BlockSpec((None, D_MODEL, IN_COLS), lambda i: (layer, 0, 0), pipeline_mode=pl.Buffered(1)),
        row(4 * LANES),
    ]
    return pl.pallas_call(
        _inproj_kernel, out_shape=out_shape, grid=(s // tm,), in_specs=in_specs, out_specs=out_specs,
        scratch_shapes=[pltpu.VMEM((QKV_A // LANES, tm, LANES), F32)],
        compiler_params=_params(("arbitrary",)), name="inproj",
    )(xb, w_in_b, rope_t)


def _pair_attention(q_pairs, k_pairs, v_pairs, valid, sinks=None):
    qs = q_pairs[0].shape[0]
    lane = lax.broadcasted_iota(jnp.int32, (qs, LANES), 1)
    lo = lane < HEAD_DIM
    q_first = (lane % HEAD_DIM) < HEAD_DIM // 2
    nt = (((1,), (1,)), ((), ()))
    scores = []
    for qp, kp in zip(q_pairs, k_pairs):
        for half in (q_first, ~q_first):
            scores.append(lax.dot_general(jnp.where(half, qp, jnp.zeros_like(qp)), kp, nt,
                                          preferred_element_type=F32))
    stats = []
    for idx, s in enumerate(scores):
        s = jnp.where(valid, s, NEG_INF)
        m = jnp.max(s, axis=-1, keepdims=True)
        if sinks is not None:
            m = jnp.maximum(m, sinks[idx])
        p = jnp.exp2(s - m)
        d = jnp.sum(p, axis=-1, keepdims=True)
        if sinks is not None:
            d = d + jnp.exp2(sinks[idx] - m)
        stats.append((p.astype(BF16), m, d))
    outs, lses = [], []
    for j, vp in enumerate(v_pairs):
        (p0, m0, d0), (p1, m1, d1) = stats[2 * j], stats[2 * j + 1]
        o0 = jnp.dot(p0, vp, preferred_element_type=F32)
        o1 = jnp.dot(p1, vp, preferred_element_type=F32)
        d = jnp.where(lo, d0, d1)
        outs.append(jnp.where(lo, o0, o1) / d)
        lses.append(jnp.where(lo, m0, m1) + jnp.log2(d))
    return outs, lses


def _fill_slab(slab_ref, prev_ref, main_ref, next_ref):
    w, bq = prev_ref.shape[0], main_ref.shape[0]
    slab_ref[0:w] = prev_ref[...]
    slab_ref[w:w + bq] = main_ref[...]
    slab_ref[w + bq:] = next_ref[...]


def _band_mask(block_row0, off, n, half_w, qs):
    kw = qs + 2 * half_w
    row = lax.broadcasted_iota(jnp.int32, (qs, kw), 0)
    col = lax.broadcasted_iota(jnp.int32, (qs, kw), 1)
    kpos = block_row0 + off - half_w + col
    return (jnp.abs(col - half_w - row) <= half_w) & (kpos >= 0) & (kpos < n)


def _attn_a_kernel(q_ref, km_ref, kp_ref, kn_ref, vm_ref, vp_ref, vn_ref, o_ref, l_ref, ks_ref, vs_ref, *, n):
    i = pl.program_id(1)
    bq = q_ref.shape[0]
    qs, w = ATT_QS, HALF_W_A
    kw = qs + 2 * w
    n_pair = WIDTH_A // LANES
    _fill_slab(ks_ref, kp_ref, km_ref, kn_ref)
    _fill_slab(vs_ref, vp_ref, vm_ref, vn_ref)

    def body(sb, carry):
        off = pl.multiple_of(sb * qs, qs)
        valid = _band_mask(i * bq, off, n, w, qs)
        q = q_ref[pl.ds(off, qs), :]
        kwin = ks_ref[pl.ds(off, kw), :]
        vwin = vs_ref[pl.ds(off, kw), :]
        pairs = [slice(j * LANES, (j + 1) * LANES) for j in range(n_pair)]
        outs, lses = _pair_attention([q[:, sl] for sl in pairs], [kwin[:, sl] for sl in pairs],
                                     [vwin[:, sl] for sl in pairs], valid)
        for sl, o, lse in zip(pairs, outs, lses):
            o_ref[pl.ds(off, qs), sl] = o.astype(BF16)
            l_ref[pl.ds(off, qs), sl] = lse
        return carry

    lax.fori_loop(0, bq // qs, body, 0)


def _halo_specs(bq, w, n, width, col):
    nb = bq // w
    main = pl.BlockSpec((None, bq, width), lambda c, i: (c, i, col))
    prev = pl.BlockSpec((None, w, width), lambda c, i: (c, jnp.maximum(i * nb - 1, 0), col))
    nxt = pl.BlockSpec((None, w, width), lambda c, i: (c, jnp.minimum((i + 1) * nb, n // w - 1), col))
    return [main, prev, nxt]


def _attn_a(qkv_r):
    r, n, _ = qkv_r.shape
    bq = min(n, 1024)
    w = HALF_W_A
    blk = lambda: pl.BlockSpec((None, bq, WIDTH_A), lambda c, i: (c, i, 0))
    in_specs = [blk()] + _halo_specs(bq, w, n, WIDTH_A, 1) + _halo_specs(bq, w, n, WIDTH_A, 2)
    out_shape = (jax.ShapeDtypeStruct((r, n, WIDTH_A), BF16), jax.ShapeDtypeStruct((r, n, WIDTH_A), F32))
    return pl.pallas_call(
        functools.partial(_attn_a_kernel, n=n), out_shape=out_shape, grid=(r, n // bq),
        in_specs=in_specs, out_specs=(blk(), blk()),
        scratch_shapes=[pltpu.VMEM((bq + 2 * w, WIDTH_A), BF16)] * 2,
        compiler_params=_params(("arbitrary",) * 2), name=f"attn_a_r{r}",
    )(*([qkv_r] * 7))


def _attn_b_kernel(sink_ref, q_ref, kvm_ref, kvp_ref, kvn_ref, o_ref, slab_ref, *, n):
    i = pl.program_id(1)
    bq = q_ref.shape[0]
    qs, w = ATT_QS, HALF_W_B
    kw = qs + 2 * w
    n_pair = WIDTH_B // LANES
    _fill_slab(slab_ref, kvp_ref, kvm_ref, kvn_ref)
    log2e = math.log2(math.e)
    sinks = [sink_ref[hh * n_pair + j] * log2e for j in range(n_pair) for hh in range(2)]

    def body(sb, carry):
        off = pl.multiple_of(sb * qs, qs)
        valid = _band_mask(i * bq, off, n, w, qs)
        q = q_ref[pl.ds(off, qs), :]
        win = slab_ref[pl.ds(off, kw), :]
        kb, vb = win[:, :LANES], win[:, LANES:]
        pairs = [slice(j * LANES, (j + 1) * LANES) for j in range(n_pair)]
        outs, _ = _pair_attention([q[:, sl] for sl in pairs], [kb] * n_pair, [vb] * n_pair, valid, sinks)
        for sl, o in zip(pairs, outs):
            o_ref[pl.ds(off, qs), sl] = o.astype(BF16)
        return carry

    lax.fori_loop(0, bq // qs, body, 0)


def _attn_b(qb, kvb, sink):
    s = qb.shape[0]
    bq = min(s, 1024)
    w = HALF_W_B
    kv_w = 2 * WIDTH_KV_B
    drop = lambda spec: pl.BlockSpec(spec.block_shape, lambda c, i, sk, f=spec.index_map: f(c, i))
    grid_spec = pltpu.PrefetchScalarGridSpec(
        num_scalar_prefetch=1, grid=(1, s // bq),
        in_specs=[pl.BlockSpec((None, bq, WIDTH_B), lambda c, i, sk: (c, i, 0))]
        + [drop(sp) for sp in _halo_specs(bq, w, s, kv_w, 0)],
        out_specs=pl.BlockSpec((None, bq, WIDTH_B), lambda c, i, sk: (c, i, 0)),
        scratch_shapes=[pltpu.VMEM((bq + 2 * w, kv_w), BF16)],
    )
    kvb3 = kvb.reshape(1, s, kv_w)
    out = pl.pallas_call(
        functools.partial(_attn_b_kernel, n=s), out_shape=jax.ShapeDtypeStruct((1, s, WIDTH_B), BF16),
        grid_spec=grid_spec, compiler_params=_params(("arbitrary",) * 2), name="attn_b",
    )(sink, qb.reshape(1, s, WIDTH_B), kvb3, kvb3, kvb3)
    return out.reshape(s, WIDTH_B)


def _merge_kernel(o1_ref, l1_ref, o4_ref, l4_ref, o16_ref, l16_ref, yb_ref, g_ref, x_ref, bg_ref,
                  wa_ref, wb_ref, wo_ref, lg_ref, lb_ref, x1_ref, x1b_ref, so4, sl4, so16, sl16):
    tm = x_ref.shape[0]
    for r, src_o, src_l, dst_o, dst_l in ((4, o4_ref, l4_ref, so4, sl4), (16, o16_ref, l16_ref, so16, sl16)):
        for c in range(r):
            for h in range(WIDTH_A // LANES):
                sl = slice(h * LANES, (h + 1) * LANES)
                dst_o[h, pl.ds(c, tm // r, stride=r), :] = src_o[c, :, sl].astype(F32)
                dst_l[h, pl.ds(c, tm // r, stride=r), :] = src_l[c, :, sl]
    slabs = lambda ref: jnp.concatenate([ref[h] for h in range(WIDTH_A // LANES)], axis=1)
    l1, l4, l16 = l1_ref[...], slabs(sl4), slabs(sl16)
    lmax = jnp.maximum(jnp.maximum(l1, l4), l16)
    e1, e4, e16 = jnp.exp2(l1 - lmax), jnp.exp2(l4 - lmax), jnp.exp2(l16 - lmax)
    y_a = (e1 * o1_ref[...].astype(F32) + e4 * slabs(so4) + e16 * slabs(so16)) / (e1 + e4 + e16)
    ta = jnp.dot(y_a.astype(BF16), wa_ref[...], preferred_element_type=F32)
    tb = jnp.dot(yb_ref[...], wb_ref[...], preferred_element_type=F32)
    gates = jax.nn.sigmoid(g_ref[...].astype(F32) + bg_ref[...])
    merged = gates[:, :D_MODEL] * ta + gates[:, D_MODEL:] * tb
    mix = jnp.dot(merged.astype(BF16), wo_ref[...], preferred_element_type=F32)
    x1 = _layer_norm(ALPHA * x_ref[...] + mix, lg_ref[...], lb_ref[...])
    x1_ref[...] = x1
    x1b_ref[...] = x1.astype(BF16)


def _merge(o1, l1, o4, l4, o16, l16, yb, g, x, b_gate, w_br_a_b, w_br_b_b, w_out_b, ln_g, ln_b, layer):
    s = x.shape[0]
    tm = MERGE_TM
    row = lambda w: pl.BlockSpec((tm, w), lambda i: (i, 0))
    perm = lambda r: pl.BlockSpec((r, tm // r, WIDTH_A), lambda i: (0, i, 0))
    vec = lambda w: pl.BlockSpec((None, 1, w), lambda i: (layer, 0, 0))
    wsp = lambda k, n: pl.BlockSpec((None, k, n), lambda i: (layer, 0, 0))
    in_specs = [row(WIDTH_A), row(WIDTH_A), perm(4), perm(4), perm(16), perm(16), row(WIDTH_B),
                row(2 * D_MODEL), row(D_MODEL), vec(2 * D_MODEL),
                wsp(WIDTH_A, D_MODEL), wsp(WIDTH_B, D_MODEL), wsp(D_MODEL, D_MODEL),
                vec(D_MODEL), vec(D_MODEL)]
    out_shape = (jax.ShapeDtypeStruct((s, D_MODEL), F32), jax.ShapeDtypeStruct((s, D_MODEL), BF16))
    return pl.pallas_call(
        _merge_kernel, out_shape=out_shape, grid=(s // tm,), in_specs=in_specs,
        out_specs=(row(D_MODEL), row(D_MODEL)),
        scratch_shapes=[pltpu.VMEM((WIDTH_A // LANES, tm, LANES), F32)] * 4,
        compiler_params=_params(("arbitrary",)), name="merge_outproj_ln",
    )(o1, l1, o4, l4, o16, l16, yb, g, x, b_gate, w_br_a_b, w_br_b_b, w_out_b, ln_g, ln_b)


def _swiglu(xb, w1, w3):
    h = jnp.dot(xb, w1, preferred_element_type=F32)
    u = jnp.dot(xb, w3, preferred_element_type=F32)
    return (h * jax.nn.sigmoid(h) * u).astype(BF16)


def _dense_kernel(xb_ref, x_ref, w1_ref, w3_ref, w2_ref, lg_ref, lb_ref, x2_ref, x2b_ref, acc_ref):
    n_chunks = w1_ref.shape[0]
    acc_ref[...] = jnp.zeros_like(acc_ref)

    def body(c, carry):
        a = _swiglu(xb_ref[...], w1_ref[c], w3_ref[c])
        acc_ref[...] += jnp.dot(a, w2_ref[c], preferred_element_type=F32)
        return carry

    lax.fori_loop(0, n_chunks, body, 0)
    x2 = _layer_norm(ALPHA * x_ref[...] + acc_ref[...], lg_ref[...], lb_ref[...])
    x2_ref[...] = x2
    x2b_ref[...] = x2.astype(BF16)


def _dense_ffn(xb, x, w1d, w3d, w2d, ln_g, ln_b, layer, j):
    s = x.shape[0]
    tm, fc = DENSE_TM, DENSE_FC
    nc = D_FF_DENSE // fc
    row = lambda: pl.BlockSpec((tm, D_MODEL), lambda i: (i, 0))
    vec = lambda: pl.BlockSpec((None, 1, D_MODEL), lambda i: (layer, 0, 0))
    resident = dict(pipeline_mode=pl.Buffered(1))
    in_specs = [row(), row(),
                pl.BlockSpec((None, nc, D_MODEL, fc), lambda i: (j, 0, 0, 0), **resident),
                pl.BlockSpec((None, nc, D_MODEL, fc), lambda i: (j, 0, 0, 0), **resident),
                pl.BlockSpec((None, nc, fc, D_MODEL), lambda i: (j, 0, 0, 0), **resident),
                vec(), vec()]
    out_shape = (jax.ShapeDtypeStruct((s, D_MODEL), F32), jax.ShapeDtypeStruct((s, D_MODEL), BF16))
    return pl.pallas_call(
        _dense_kernel, out_shape=out_shape, grid=(s // tm,), in_specs=in_specs,
        out_specs=(row(), row()), scratch_shapes=[pltpu.VMEM((tm, D_MODEL), F32)],
        compiler_params=_params(("arbitrary",)), name="dense_ffn_ln",
    )(xb, x, w1d, w3d, w2d, ln_g, ln_b)


def _router_kernel(x_ref, wr_ref, idx_ref, gate_ref, rank_ref, cnt_ref, carry_ref):
    i = pl.program_id(0)
    tm = x_ref.shape[0]

    @pl.when(i == 0)
    def _():
        carry_ref[...] = jnp.zeros_like(carry_ref)

    logits = jnp.dot(x_ref[...], wr_ref[...], preferred_element_type=F32, precision=lax.Precision.HIGHEST)
    lt = logits.T[:N_EXPERTS, :]
    ii = lax.broadcasted_iota(jnp.int32, lt.shape, 0)
    m1 = jnp.max(lt, axis=0, keepdims=True)
    i1 = jnp.min(jnp.where(lt == m1, ii, N_EXPERTS), axis=0, keepdims=True)
    l2 = jnp.where(ii == i1, -jnp.inf, lt)
    m2 = jnp.max(l2, axis=0, keepdims=True)
    i2 = jnp.min(jnp.where(l2 == m2, ii, N_EXPERTS), axis=0, keepdims=True)
    e = jnp.exp(m2 - m1)
    gate_ref[0:1, :] = 1.0 / (1.0 + e)
    gate_ref[1:2, :] = e / (1.0 + e)
    idx_ref[0:1, :] = i1
    idx_ref[1:2, :] = i2
    member = (ii == i1) | (ii == i2)
    a = lax.broadcasted_iota(jnp.int32, (tm, tm), 0)
    b = lax.broadcasted_iota(jnp.int32, (tm, tm), 1)
    upper = jnp.where(a < b, 1.0, 0.0).astype(BF16)
    memf = jnp.where(member, 1.0, 0.0)
    rank = jnp.dot(memf.astype(BF16), upper, preferred_element_type=F32) + carry_ref[:, 0:1]
    rank_ref[0:1, :] = jnp.sum(jnp.where(ii == i1, rank, 0.0), axis=0, keepdims=True).astype(jnp.int32)
    rank_ref[1:2, :] = jnp.sum(jnp.where(ii == i2, rank, 0.0), axis=0, keepdims=True).astype(jnp.int32)
    carry_ref[...] = carry_ref[...] + jnp.sum(memf, axis=1, keepdims=True)
    cnt_ref[...] = carry_ref[...].astype(jnp.int32)


def _router(x, wr_pad):
    t = x.shape[0]
    tm = ROUTE_TM
    two = lambda: pl.BlockSpec((2, tm), lambda i: (0, i))
    out_shape = (jax.ShapeDtypeStruct((2, t), jnp.int32), jax.ShapeDtypeStruct((2, t), F32),
                 jax.ShapeDtypeStruct((2, t), jnp.int32), jax.ShapeDtypeStruct((N_EXPERTS, LANES), jnp.int32))
    return pl.pallas_call(
        _router_kernel, out_shape=out_shape, grid=(t // tm,),
        in_specs=[pl.BlockSpec((tm, D_MODEL), lambda i: (i, 0)),
                  pl.BlockSpec((D_MODEL, LANES), lambda i: (0, 0))],
        out_specs=(two(), two(), two(), pl.BlockSpec((N_EXPERTS, LANES), lambda i: (0, 0))),
        scratch_shapes=[pltpu.VMEM((N_EXPERTS, LANES), F32)],
        compiler_params=_params(("arbitrary",)), name="router",
    )(x, wr_pad)


def _row_copy(src_ref, src_row, dst_ref, dst_row, sem):
    return pltpu.make_async_copy(src_ref.at[pl.ds(src_row, 1)], dst_ref.at[pl.ds(dst_row, 1)], sem)


def _scatter_kernel(dest_ref, x_ref, zeros_ref, xbuf_ref, stage_ref, sems):
    del zeros_ref
    tm = x_ref.shape[0]
    i, n = pl.program_id(0), pl.num_programs(0)
    t = n * tm
    slot = i % 2
    stage_ref[slot] = x_ref[...]

    def start(r, carry):
        for k in range(2):
            _row_copy(stage_ref.at[slot], r, xbuf_ref, dest_ref[k * t + i * tm + r], sems.at[slot]).start(priority=k)
        return carry

    def drain(s):
        def wait(r, carry):
            for k in range(2):
                _row_copy(stage_ref.at[s], 0, xbuf_ref, 0, sems.at[s]).wait()
            return carry
        lax.fori_loop(0, tm, wait, 0, unroll=DMA_UNROLL)

    lax.fori_loop(0, tm, start, 0, unroll=DMA_UNROLL)

    @pl.when(i > 0)
    def _():
        drain(1 - slot)

    @pl.when(i == n - 1)
    def _():
        drain(slot)


def _scatter_rows(dest, x, n_rows):
    t = x.shape[0]
    tm = SCATTER_TM
    zeros = jnp.zeros((n_rows, D_MODEL), x.dtype)
    grid_spec = pltpu.PrefetchScalarGridSpec(
        num_scalar_prefetch=1, grid=(t // tm,),
        in_specs=[pl.BlockSpec((tm, D_MODEL), lambda i, d: (i, 0)), pl.BlockSpec(memory_space=pl.ANY)],
        out_specs=pl.BlockSpec(memory_space=pl.ANY),
        scratch_shapes=[pltpu.VMEM((2, tm, D_MODEL), x.dtype), pltpu.SemaphoreType.DMA((2,))],
    )
    return pl.pallas_call(
        _scatter_kernel, out_shape=jax.ShapeDtypeStruct((n_rows, D_MODEL), x.dtype), grid_spec=grid_spec,
        input_output_aliases={2: 0},
        compiler_params=pltpu.CompilerParams(dimension_semantics=("arbitrary",), has_side_effects=True,
                                             disable_bounds_checks=True),
        name="moe_scatter",
    )(dest, x, zeros)


def _expert_kernel(be_ref, nu_ref, x_ref, w1_ref, w3_ref, w2_ref, y_ref, xb_ref, acc_ref):
    i, c = pl.program_id(0), pl.program_id(1)

    @pl.when(i < nu_ref[0])
    def _():
        @pl.when(c == 0)
        def _():
            xb_ref[...] = x_ref[...].astype(BF16)
            acc_ref[...] = jnp.zeros_like(acc_ref)

        for k in range(MOE_FC // MOE_SUB):
            sl = slice(k * MOE_SUB, (k + 1) * MOE_SUB)
            a = _swiglu(xb_ref[...], w1_ref[:, sl], w3_ref[:, sl])
            acc_ref[...] += jnp.dot(a, w2_ref[sl, :], preferred_element_type=F32)

        @pl.when(c == pl.num_programs(1) - 1)
        def _():
            y_ref[...] = acc_ref[...]

    @pl.when(i >= nu_ref[0])
    def _():
        y_ref[...] = jnp.zeros_like(y_ref)


def _expert_ffn(block_expert, n_used, xbuf, w1e, w3e, w2e, j):
    n_rows = xbuf.shape[0]
    bm, fc = MOE_BM, MOE_FC
    nc = D_FF_EXPERT // fc
    row = lambda i, be, nu: jnp.minimum(i, jnp.maximum(nu[0] - 1, 0))
    col = lambda i, c, nu: jnp.where(i < nu[0], c, nc - 1)
    grid_spec = pltpu.PrefetchScalarGridSpec(
        num_scalar_prefetch=2, grid=(n_rows // bm, nc),
        in_specs=[
            pl.BlockSpec((bm, D_MODEL), lambda i, c, be, nu: (row(i, be, nu), 0)),
            pl.BlockSpec((None, None, D_MODEL, fc), lambda i, c, be, nu: (j, be[row(i, be, nu)], 0, col(i, c, nu))),
            pl.BlockSpec((None, None, D_MODEL, fc), lambda i, c, be, nu: (j, be[row(i, be, nu)], 0, col(i, c, nu))),
            pl.BlockSpec((None, None, fc, D_MODEL), lambda i, c, be, nu: (j, be[row(i, be, nu)], col(i, c, nu), 0)),
        ],
        out_specs=pl.BlockSpec((bm, D_MODEL), lambda i, c, be, nu: (i, 0)),
        scratch_shapes=[pltpu.VMEM((bm, D_MODEL), BF16), pltpu.VMEM((bm, D_MODEL), F32)],
    )
    return pl.pallas_call(
        _expert_kernel, out_shape=jax.ShapeDtypeStruct((n_rows, D_MODEL), F32), grid_spec=grid_spec,
        compiler_params=_params(("arbitrary", "arbitrary")), name="expert_ffn",
    )(block_expert, n_used, xbuf, w1e, w3e, w2e)


def _combine_kernel(dest_ref, ybuf_ref, gate_ref, x_ref, lg_ref, lb_ref, x2_ref, x2b_ref, buf_ref, sems):
    tm = x_ref.shape[0]
    i, n = pl.program_id(0), pl.num_programs(0)
    t = n * tm
    slot = i % 2

    def issue(step, s):
        def start(r, carry):
            for k in range(2):
                _row_copy(ybuf_ref, dest_ref[k * t + step * tm + r], buf_ref.at[s, k], r, sems.at[s]).start(priority=k)
            return carry
        lax.fori_loop(0, tm, start, 0, unroll=DMA_UNROLL)

    @pl.when(i == 0)
    def _():
        issue(0, 0)

    @pl.when(i + 1 < n)
    def _():
        issue(i + 1, 1 - slot)

    def wait(r, carry):
        for k in range(2):
            _row_copy(ybuf_ref, 0, buf_ref.at[slot, k], 0, sems.at[slot]).wait()
        return carry

    lax.fori_loop(0, tm, wait, 0, unroll=DMA_UNROLL)
    g = gate_ref[...]
    y = buf_ref[slot, 0] * g[:, 0:1] + buf_ref[slot, 1] * g[:, 1:2]
    x2 = _layer_norm(ALPHA * x_ref[...] + y, lg_ref[...], lb_ref[...])
    x2_ref[...] = x2
    x2b_ref[...] = x2.astype(BF16)


def _combine(dest, ybuf, gates_t, x, ln_g, ln_b, layer):
    t = x.shape[0]
    tm = COMBINE_TM
    row = lambda: pl.BlockSpec((tm, D_MODEL), lambda i, d: (i, 0))
    vec = lambda: pl.BlockSpec((None, 1, D_MODEL), lambda i, d: (layer, 0, 0))
    grid_spec = pltpu.PrefetchScalarGridSpec(
        num_scalar_prefetch=1, grid=(t // tm,),
        in_specs=[pl.BlockSpec(memory_space=pl.ANY), pl.BlockSpec((tm, 2), lambda i, d: (i, 0)), row(), vec(), vec()],
        out_specs=(row(), row()),
        scratch_shapes=[pltpu.VMEM((2, 2, tm, D_MODEL), F32), pltpu.SemaphoreType.DMA((2,))],
    )
    out_shape = (jax.ShapeDtypeStruct((t, D_MODEL), F32), jax.ShapeDtypeStruct((t, D_MODEL), BF16))
    return pl.pallas_call(
        _combine_kernel, out_shape=out_shape, grid_spec=grid_spec,
        compiler_params=pltpu.CompilerParams(dimension_semantics=("arbitrary",), vmem_limit_bytes=VMEM_LIMIT,
                                             disable_bounds_checks=True),
        name="moe_combine_ln",
    )(dest, ybuf, gates_t, x, ln_g, ln_b)


def _moe_ffn(x, wr_pad, w1e, w3e, w2e, ln_g, ln_b, layer, j):
    t = x.shape[0]
    bm = MOE_BM
    idx, gates, rank, cnt = _router(x, wr_pad)
    counts = cnt[:, 0]
    padded = (counts + bm - 1) // bm * bm
    pad_ends = jnp.cumsum(padded)
    pad_starts = pad_ends - padded
    n_rows = -(-(2 * t + N_EXPERTS * (bm - 1)) // bm) * bm
    n_blocks = n_rows // bm
    expert_ids = jnp.arange(N_EXPERTS, dtype=jnp.int32)[:, None, None]
    start_of = jnp.sum(jnp.where(idx[None] == expert_ids, pad_starts[:, None, None], 0), axis=0)
    dest = (start_of + rank).reshape(-1).astype(jnp.int32)
    block_start = jnp.arange(n_blocks, dtype=jnp.int32) * bm
    block_expert = jnp.minimum(jnp.sum(block_start[:, None] >= pad_ends[None, :], axis=1), N_EXPERTS - 1)
    n_used = (pad_ends[-1:] // bm).astype(jnp.int32)
    xbuf = _scatter_rows(dest, x, n_rows)
    ybuf = _expert_ffn(block_expert.astype(jnp.int32), n_used, xbuf, w1e, w3e, w2e, j)
    return _combine(dest, ybuf, gates.T, x, ln_g, ln_b, layer)


def _rotary_tables(seq):
    inv_freq = ROPE_THETA ** (-jnp.arange(0, HEAD_DIM, 2, dtype=F32) / HEAD_DIM)
    ang = jnp.arange(seq, dtype=F32)[:, None] * inv_freq[None, :]
    cos, sin = jnp.cos(ang), jnp.sin(ang)
    cos_t, sin_t = jnp.tile(cos, (1, 4)), jnp.concatenate([-sin, -sin, sin, sin], axis=1)
    q_scale = HEAD_DIM ** -0.5 * math.log2(math.e)
    return jnp.concatenate([cos_t * q_scale, sin_t * q_scale, cos_t, sin_t], axis=1)


_B_HEAD_ORDER = tuple(h for j in range(N_HEADS_B // N_KV_B) for h in (j, N_HEADS_B // N_KV_B + j))


def kernel(x, w_in, b_gate, sink_logits, w_br_a, w_br_b, w_out, ln_mix_g, ln_mix_b, w1_dense, w3_dense,
           w2_dense, w_router, w1_exp, w3_exp, w2_exp, ln_ffn_g, ln_ffn_b):
    b, s, d = x.shape
    assert b == 1 and d == D_MODEL and s % (16 * 2 * ATT_QS) == 0 and s % IN_TM == 0
    rope_t = _rotary_tables(s)
    order = jnp.asarray(_B_HEAD_ORDER)

    def rotary_layout(w, head_order=None):
        heads = w.shape[-1] // HEAD_DIM
        w = w.reshape(DEPTH, D_MODEL, heads, 2, HEAD_DIM // 2)
        if head_order is not None:
            w = w[:, :, head_order]
        w = w.reshape(DEPTH, D_MODEL, heads // 2, 2, 2, HEAD_DIM // 2).transpose(0, 1, 2, 4, 3, 5)
        return w.reshape(DEPTH, D_MODEL, heads * HEAD_DIM)

    kb0 = QKV_A + WIDTH_B
    w_in_b = jnp.concatenate([
        rotary_layout(w_in[:, :, :WIDTH_A]), rotary_layout(w_in[:, :, WIDTH_A:2 * WIDTH_A]),
        w_in[:, :, 2 * WIDTH_A:QKV_A], rotary_layout(w_in[:, :, QKV_A:kb0], order),
        rotary_layout(w_in[:, :, kb0:kb0 + WIDTH_KV_B]), w_in[:, :, kb0 + WIDTH_KV_B:]], axis=2).astype(BF16)
    w_br_b_b = w_br_b.reshape(DEPTH, N_HEADS_B, HEAD_DIM, D_MODEL)[:, order].reshape(w_br_b.shape).astype(BF16)
    w_br_a_b, w_out_b = w_br_a.astype(BF16), w_out.astype(BF16)
    n_dense, nc = w1_dense.shape[0], D_FF_DENSE // DENSE_FC
    chunk_cols = lambda w: w.astype(BF16).reshape(n_dense, D_MODEL, nc, DENSE_FC).transpose(0, 2, 1, 3)
    w1d, w3d = chunk_cols(w1_dense), chunk_cols(w3_dense)
    w2d = w2_dense.astype(BF16).reshape(n_dense, nc, DENSE_FC, D_MODEL)
    w1e, w3e, w2e = w1_exp.astype(BF16), w3_exp.astype(BF16), w2_exp.astype(BF16)
    wr_pad = jnp.pad(w_router, ((0, 0), (0, 0), (0, LANES - N_EXPERTS)))
    vec3 = lambda v: v.reshape(v.shape[0], 1, v.shape[1])
    b_gate3, lmg, lmb, lfg, lfb = map(vec3, (b_gate, ln_mix_g, ln_mix_b, ln_ffn_g, ln_ffn_b))

    xf = x.reshape(s, d)
    xb = xf.astype(BF16)
    for i in range(DEPTH):
        qkv1, qkv4, qkv16, qb, kvb, g = _inproj(xb, w_in_b, i, rope_t)
        o1, l1 = _attn_a(qkv1.reshape(1, s, QKV_A))
        o4, l4 = _attn_a(qkv4)
        o16, l16 = _attn_a(qkv16)
        yb = _attn_b(qb, kvb, sink_logits[i])
        xf, xb = _merge(o1.reshape(s, WIDTH_A), l1.reshape(s, WIDTH_A), o4, l4, o16, l16, yb, g, xf,
                        b_gate3, w_br_a_b, w_br_b_b, w_out_b, lmg, lmb, i)
        j = i // 2
        if i % 2 == 0:
            xf, xb = _dense_ffn(xb, xf, w1d, w3d, w2d, lfg, lfb, i, j)
        else:
            xf, xb = _moe_ffn(xf, wr_pad[j], w1e, w3e, w2e, lfg, lfb, i, j)
    return xf.reshape(b, s, d)
```

```python
import functools
import math

import jax
import jax.numpy as jnp
from jax import lax
from jax.experimental import pallas as pl
from jax.experimental.pallas import tpu as pltpu

F32 = jnp.float32
BF16 = jnp.bfloat16

D_MODEL = 1024
DEPTH = 4
HEAD_DIM = 64
N_HEADS_A = 8
DILATIONS = (1, 4, 16)
HALF_W_A = 64
N_HEADS_B = 8
N_KV_B = 2
HALF_W_B = 128
ROPE_THETA = 10000.0
WIDTH_A = N_HEADS_A * HEAD_DIM
WIDTH_B = N_HEADS_B * HEAD_DIM
WIDTH_KV_B = N_KV_B * HEAD_DIM
QKV_A = 3 * WIDTH_A
IN_COLS = QKV_A + WIDTH_B + 2 * WIDTH_KV_B + 2 * D_MODEL
D_FF_DENSE = 2816
N_EXPERTS = 8
D_FF_EXPERT = 3584
ALPHA = (2 * DEPTH) ** 0.25
LN_EPS = 1e-5
NEG_INF = -1e30

LANES = 128
VMEM_LIMIT = 56 * 1024 * 1024

IN_TM, IN_TN = 512, 256
ATT_QS = 128
ATT_A_UNROLL, ATT_B_UNROLL = 8, 2
MERGE_TM = 512
DENSE_TM, DENSE_FC = 1024, 256
ROUTE_TM = 512
MOE_BM = 512
MOE_FC = 1792
MOE_SUB = 256
SCATTER_TM = 512
COMBINE_TM = 512
DMA_UNROLL = 8


def _params(sem):
    return pltpu.CompilerParams(dimension_semantics=sem, vmem_limit_bytes=VMEM_LIMIT)


def _layer_norm(z, g, b):
    mu = jnp.mean(z, axis=-1, keepdims=True)
    zc = z - mu
    var = jnp.mean(zc * zc, axis=-1, keepdims=True)
    return zc * lax.rsqrt(var + LN_EPS) * g + b


def _rope(a, cos, sin):
    return a * cos + pltpu.roll(a, LANES // 2, axis=1) * sin


def _inproj_kernel(x_ref, w_ref, rope_ref, qkv1_ref, qkv4_ref, qkv16_ref, qb_ref, kvb_ref, g_ref, rs_ref):
    tm = x_ref.shape[0]
    x = x_ref[...]
    rope_q = lambda v: _rope(v, rope_ref[:, 0:LANES], rope_ref[:, LANES:2 * LANES])
    rope_k = lambda v: _rope(v, rope_ref[:, 2 * LANES:3 * LANES], rope_ref[:, 3 * LANES:])
    plain = lambda v: v
    n_q, n_a = WIDTH_A // LANES, QKV_A // LANES
    qb0 = n_a
    kb0 = qb0 + WIDTH_B // LANES
    vb0 = kb0 + WIDTH_KV_B // LANES
    g0 = vb0 + WIDTH_KV_B // LANES
    per_dot = IN_TN // LANES

    for t in range(IN_COLS // IN_TN):
        acc = jnp.dot(x, w_ref[:, t * IN_TN:(t + 1) * IN_TN], preferred_element_type=F32)
        for h in range(per_dot):
            c = t * per_dot + h
            blk = acc[:, h * LANES:(h + 1) * LANES]
            if c < n_a:
                post = rope_q if c < n_q else rope_k if c < 2 * n_q else plain
                sl = slice(c * LANES, (c + 1) * LANES)
                rs_ref[c] = post(blk)
                qkv1_ref[:, sl] = rs_ref[c].astype(BF16)
                for r, ref in ((4, qkv4_ref), (16, qkv16_ref)):
                    for res in range(r):
                        ref[res, :, sl] = rs_ref[c, pl.ds(res, tm // r, stride=r), :].astype(BF16)
            elif c < kb0:
                qb_ref[:, (c - qb0) * LANES:(c - qb0 + 1) * LANES] = rope_q(blk).astype(BF16)
            elif c < vb0:
                kvb_ref[:, :LANES] = rope_k(blk).astype(BF16)
            elif c < g0:
                kvb_ref[:, LANES:] = blk.astype(BF16)
            else:
                g_ref[:, (c - g0) * LANES:(c - g0 + 1) * LANES] = blk.astype(BF16)


def _inproj(xb, w_in_b, layer, rope_t):
    s = xb.shape[0]
    tm = IN_TM
    out_shape = (
        jax.ShapeDtypeStruct((s, QKV_A), BF16),
        jax.ShapeDtypeStruct((4, s // 4, QKV_A), BF16),
        jax.ShapeDtypeStruct((16, s // 16, QKV_A), BF16),
        jax.ShapeDtypeStruct((s, WIDTH_B), BF16),
        jax.ShapeDtypeStruct((s, 2 * WIDTH_KV_B), BF16),
        jax.ShapeDtypeStruct((s, 2 * D_MODEL), BF16),
    )
    row = lambda w: pl.BlockSpec((tm, w), lambda i: (i, 0))
    out_specs = (
        row(QKV_A),
        pl.BlockSpec((4, tm // 4, QKV_A), lambda i: (0, i, 0)),
        pl.BlockSpec((16, tm // 16, QKV_A), lambda i: (0, i, 0)),
        row(WIDTH_B), row(2 * WIDTH_KV_B), row(2 * D_MODEL),
    )
    in_specs = [
        row(D_MODEL),
        pl.BlockSpec((None, D_MODEL, IN_COLS), lambda i: (layer, 0, 0), pipeline_mode=pl.Buffered(1)),
        row(4 * LANES),
    ]
    return pl.pallas_call(
        _inproj_kernel, out_shape=out_shape, grid=(s // tm,), in_specs=in_specs, out_specs=out_specs,
        scratch_shapes=[pltpu.VMEM((QKV_A // LANES, tm, LANES), F32)],
        compiler_params=_params(("arbitrary",)), name="inproj",
    )(xb, w_in_b, rope_t)


def _pair_attention(q_pairs, k_pairs, v_pairs, valid, sinks=None):
    qs = q_pairs[0].shape[0]
    lane = lax.broadcasted_iota(jnp.int32, (qs, LANES), 1)
    lo = lane < HEAD_DIM
    q_first = (lane % HEAD_DIM) < HEAD_DIM // 2
    nt = (((1,), (1,)), ((), ()))
    scores = []
    for qp, kp in zip(q_pairs, k_pairs):
        for half in (q_first, ~q_first):
            scores.append(lax.dot_general(jnp.where(half, qp, jnp.zeros_like(qp)), kp, nt,
                                          preferred_element_type=F32))
    stats = []
    for idx, s in enumerate(scores):
        s = jnp.where(valid, s, NEG_INF)
        m = jnp.max(s, axis=-1, keepdims=True)
        if sinks is not None:
            m = jnp.maximum(m, sinks[idx])
        p = jnp.exp2(s - m)
        d = jnp.sum(p, axis=-1, keepdims=True)
        if sinks is not None:
            d = d + jnp.exp2(sinks[idx] - m)
        stats.append((p.astype(BF16), m, d))
    outs, lses = [], []
    for j, vp in enumerate(v_pairs):
        (p0, m0, d0), (p1, m1, d1) = stats[2 * j], stats[2 * j + 1]
        o0 = jnp.dot(p0, vp, preferred_element_type=F32)
        o1 = jnp.dot(p1, vp, preferred_element_type=F32)
        d = jnp.where(lo, d0, d1)
        outs.append(jnp.where(lo, o0, o1) / d)
        lses.append(jnp.where(lo, m0, m1) + jnp.log2(d))
    return outs, lses


def _fill_slab(slab_ref, prev_ref, main_ref, next_ref):
    w, bq = prev_ref.shape[0], main_ref.shape[0]
    slab_ref[0:w] = prev_ref[...]
    slab_ref[w:w + bq] = main_ref[...]
    slab_ref[w + bq:] = next_ref[...]


def _band_mask(block_row0, off, n, half_w, qs):
    kw = qs + 2 * half_w
    row = lax.broadcasted_iota(jnp.int32, (qs, kw), 0)
    col = lax.broadcasted_iota(jnp.int32, (qs, kw), 1)
    kpos = block_row0 + off - half_w + col
    return (jnp.abs(col - half_w - row) <= half_w) & (kpos >= 0) & (kpos < n)


def _attn_a_kernel(q_ref, km_ref, kp_ref, kn_ref, vm_ref, vp_ref, vn_ref, o_ref, l_ref, ks_ref, vs_ref, *, n):
    i = pl.program_id(1)
    bq = q_ref.shape[0]
    qs, w = ATT_QS, HALF_W_A
    kw = qs + 2 * w
    n_pair = WIDTH_A // LANES
    _fill_slab(ks_ref, kp_ref, km_ref, kn_ref)
    _fill_slab(vs_ref, vp_ref, vm_ref, vn_ref)

    def body(sb, carry):
        off = pl.multiple_of(sb * qs, qs)
        valid = _band_mask(i * bq, off, n, w, qs)
        q = q_ref[pl.ds(off, qs), :]
        kwin = ks_ref[pl.ds(off, kw), :]
        vwin = vs_ref[pl.ds(off, kw), :]
        pairs = [slice(j * LANES, (j + 1) * LANES) for j in range(n_pair)]
        outs, lses = _pair_attention([q[:, sl] for sl in pairs], [kwin[:, sl] for sl in pairs],
                                     [vwin[:, sl] for sl in pairs], valid)
        for sl, o, lse in zip(pairs, outs, lses):
            o_ref[pl.ds(off, qs), sl] = o.astype(BF16)
            l_ref[pl.ds(off, qs), sl] = lse
        return carry

    lax.fori_loop(0, bq // qs, body, 0, unroll=ATT_A_UNROLL)


def _halo_specs(bq, w, n, width, col):
    nb = bq // w
    main = pl.BlockSpec((None, bq, width), lambda c, i: (c, i, col))
    prev = pl.BlockSpec((None, w, width), lambda c, i: (c, jnp.maximum(i * nb - 1, 0), col))
    nxt = pl.BlockSpec((None, w, width), lambda c, i: (c, jnp.minimum((i + 1) * nb, n // w - 1), col))
    return [main, prev, nxt]


def _attn_a(qkv_r):
    r, n, _ = qkv_r.shape
    bq = min(n, 1024)
    w = HALF_W_A
    blk = lambda: pl.BlockSpec((None, bq, WIDTH_A), lambda c, i: (c, i, 0))
    in_specs = [blk()] + _halo_specs(bq, w, n, WIDTH_A, 1) + _halo_specs(bq, w, n, WIDTH_A, 2)
    out_shape = (jax.ShapeDtypeStruct((r, n, WIDTH_A), BF16), jax.ShapeDtypeStruct((r, n, WIDTH_A), F32))
    return pl.pallas_call(
        functools.partial(_attn_a_kernel, n=n), out_shape=out_shape, grid=(r, n // bq),
        in_specs=in_specs, out_specs=(blk(), blk()),
        scratch_shapes=[pltpu.VMEM((bq + 2 * w, WIDTH_A), BF16)] * 2,
        compiler_params=_params(("arbitrary",) * 2), name=f"attn_a_r{r}",
    )(*([qkv_r] * 7))


def _attn_b_kernel(sink_ref, q_ref, kvm_ref, kvp_ref, kvn_ref, o_ref, slab_ref, *, n):
    i = pl.program_id(1)
    bq = q_ref.shape[0]
    qs, w = ATT_QS, HALF_W_B
    kw = qs + 2 * w
    n_pair = WIDTH_B // LANES
    _fill_slab(slab_ref, kvp_ref, kvm_ref, kvn_ref)
    log2e = math.log2(math.e)
    sinks = [sink_ref[hh * n_pair + j] * log2e for j in range(n_pair) for hh in range(2)]

    def body(sb, carry):
        off = pl.multiple_of(sb * qs, qs)
        valid = _band_mask(i * bq, off, n, w, qs)
        q = q_ref[pl.ds(off, qs), :]
        win = slab_ref[pl.ds(off, kw), :]
        kb, vb = win[:, :LANES], win[:, LANES:]
        pairs = [slice(j * LANES, (j + 1) * LANES) for j in range(n_pair)]
        outs, _ = _pair_attention([q[:, sl] for sl in pairs], [kb] * n_pair, [vb] * n_pair, valid, sinks)
        for sl, o in zip(pairs, outs):
            o_ref[pl.ds(off, qs), sl] = o.astype(BF16)
        return carry

    lax.fori_loop(0, bq // qs, body, 0, unroll=ATT_B_UNROLL)


def _attn_b(qb, kvb, sink):
    s = qb.shape[0]
    bq = min(s, 1024)
    w = HALF_W_B
    kv_w = 2 * WIDTH_KV_B
    drop = lambda spec: pl.BlockSpec(spec.block_shape, lambda c, i, sk, f=spec.index_map: f(c, i))
    grid_spec = pltpu.PrefetchScalarGridSpec(
        num_scalar_prefetch=1, grid=(1, s // bq),
        in_specs=[pl.BlockSpec((None, bq, WIDTH_B), lambda c, i, sk: (c, i, 0))]
        + [drop(sp) for sp in _halo_specs(bq, w, s, kv_w, 0)],
        out_specs=pl.BlockSpec((None, bq, WIDTH_B), lambda c, i, sk: (c, i, 0)),
        scratch_shapes=[pltpu.VMEM((bq + 2 * w, kv_w), BF16)],
    )
    kvb3 = kvb.reshape(1, s, kv_w)
    out = pl.pallas_call(
        functools.partial(_attn_b_kernel, n=s), out_shape=jax.ShapeDtypeStruct((1, s, WIDTH_B), BF16),
        grid_spec=grid_spec, compiler_params=_params(("arbitrary",) * 2), name="attn_b",
    )(sink, qb.reshape(1, s, WIDTH_B), kvb3, kvb3, kvb3)
    return out.reshape(s, WIDTH_B)


def _merge_kernel(o1_ref, l1_ref, o4_ref, l4_ref, o16_ref, l16_ref, yb_ref, g_ref, x_ref, bg_ref,
                  wa_ref, wb_ref, wo_ref, lg_ref, lb_ref, x1_ref, x1b_ref, so4, sl4, so16, sl16):
    tm = x_ref.shape[0]
    for r, src_o, src_l, dst_o, dst_l in ((4, o4_ref, l4_ref, so4, sl4), (16, o16_ref, l16_ref, so16, sl16)):
        for c in range(r):
            for h in range(WIDTH_A // LANES):
                sl = slice(h * LANES, (h + 1) * LANES)
                dst_o[h, pl.ds(c, tm // r, stride=r), :] = src_o[c, :, sl].astype(F32)
                dst_l[h, pl.ds(c, tm // r, stride=r), :] = src_l[c, :, sl]
    slabs = lambda ref: jnp.concatenate([ref[h] for h in range(WIDTH_A // LANES)], axis=1)
    l1, l4, l16 = l1_ref[...], slabs(sl4), slabs(sl16)
    lmax = jnp.maximum(jnp.maximum(l1, l4), l16)
    e1, e4, e16 = jnp.exp2(l1 - lmax), jnp.exp2(l4 - lmax), jnp.exp2(l16 - lmax)
    y_a = (e1 * o1_ref[...].astype(F32) + e4 * slabs(so4) + e16 * slabs(so16)) / (e1 + e4 + e16)
    ta = jnp.dot(y_a.astype(BF16), wa_ref[...], preferred_element_type=F32)
    tb = jnp.dot(yb_ref[...], wb_ref[...], preferred_element_type=F32)
    gates = jax.nn.sigmoid(g_ref[...].astype(F32) + bg_ref[...])
    merged = gates[:, :D_MODEL] * ta + gates[:, D_MODEL:] * tb
    mix = jnp.dot(merged.astype(BF16), wo_ref[...], preferred_element_type=F32)
    x1 = _layer_norm(ALPHA * x_ref[...] + mix, lg_ref[...], lb_ref[...])
    x1_ref[...] = x1
    x1b_ref[...] = x1.astype(BF16)


def _merge(o1, l1, o4, l4, o16, l16, yb, g, x, b_gate, w_br_a_b, w_br_b_b, w_out_b, ln_g, ln_b, layer):
    s = x.shape[0]
    tm = MERGE_TM
    row = lambda w: pl.BlockSpec((tm, w), lambda i: (i, 0))
    perm = lambda r: pl.BlockSpec((r, tm // r, WIDTH_A), lambda i: (0, i, 0))
    vec = lambda w: pl.BlockSpec((None, 1, w), lambda i: (layer, 0, 0))
    wsp = lambda k, n: pl.BlockSpec((None, k, n), lambda i: (layer, 0, 0))
    in_specs = [row(WIDTH_A), row(WIDTH_A), perm(4), perm(4), perm(16), perm(16), row(WIDTH_B),
                row(2 * D_MODEL), row(D_MODEL), vec(2 * D_MODEL),
                wsp(WIDTH_A, D_MODEL), wsp(WIDTH_B, D_MODEL), wsp(D_MODEL, D_MODEL),
                vec(D_MODEL), vec(D_MODEL)]
    out_shape = (jax.ShapeDtypeStruct((s, D_MODEL), F32), jax.ShapeDtypeStruct((s, D_MODEL), BF16))
    return pl.pallas_call(
        _merge_kernel, out_shape=out_shape, grid=(s // tm,), in_specs=in_specs,
        out_specs=(row(D_MODEL), row(D_MODEL)),
        scratch_shapes=[pltpu.VMEM((WIDTH_A // LANES, tm, LANES), F32)] * 4,
        compiler_params=_params(("arbitrary",)), name="merge_outproj_ln",
    )(o1, l1, o4, l4, o16, l16, yb, g, x, b_gate, w_br_a_b, w_br_b_b, w_out_b, ln_g, ln_b)


def _swiglu(xb, w1, w3):
    h = jnp.dot(xb, w1, preferred_element_type=F32)
    u = jnp.dot(xb, w3, preferred_element_type=F32)
    return (h * jax.nn.sigmoid(h) * u).astype(BF16)


def _dense_kernel(xb_ref, x_ref, w1_ref, w3_ref, w2_ref, lg_ref, lb_ref, x2_ref, x2b_ref, acc_ref):
    n_chunks = w1_ref.shape[0]
    acc_ref[...] = jnp.zeros_like(acc_ref)

    def body(c, carry):
        a = _swiglu(xb_ref[...], w1_ref[c], w3_ref[c])
        acc_ref[...] += jnp.dot(a, w2_ref[c], preferred_element_type=F32)
        return carry

    lax.fori_loop(0, n_chunks, body, 0)
    x2 = _layer_norm(ALPHA * x_ref[...] + acc_ref[...], lg_ref[...], lb_ref[...])
    x2_ref[...] = x2
    x2b_ref[...] = x2.astype(BF16)


def _dense_ffn(xb, x, w1d, w3d, w2d, ln_g, ln_b, layer, j):
    s = x.shape[0]
    tm, fc = DENSE_TM, DENSE_FC
    nc = D_FF_DENSE // fc
    row = lambda: pl.BlockSpec((tm, D_MODEL), lambda i: (i, 0))
    vec = lambda: pl.BlockSpec((None, 1, D_MODEL), lambda i: (layer, 0, 0))
    resident = dict(pipeline_mode=pl.Buffered(1))
    in_specs = [row(), row(),
                pl.BlockSpec((None, nc, D_MODEL, fc), lambda i: (j, 0, 0, 0), **resident),
                pl.BlockSpec((None, nc, D_MODEL, fc), lambda i: (j, 0, 0, 0), **resident),
                pl.BlockSpec((None, nc, fc, D_MODEL), lambda i: (j, 0, 0, 0), **resident),
                vec(), vec()]
    out_shape = (jax.ShapeDtypeStruct((s, D_MODEL), F32), jax.ShapeDtypeStruct((s, D_MODEL), BF16))
    return pl.pallas_call(
        _dense_kernel, out_shape=out_shape, grid=(s // tm,), in_specs=in_specs,
        out_specs=(row(), row()), scratch_shapes=[pltpu.VMEM((tm, D_MODEL), F32)],
        compiler_params=_params(("arbitrary",)), name="dense_ffn_ln",
    )(xb, x, w1d, w3d, w2d, ln_g, ln_b)


def _router_kernel(x_ref, wr_ref, idx_ref, gate_ref, rank_ref, cnt_ref, carry_ref):
    i = pl.program_id(0)
    tm = x_ref.shape[0]

    @pl.when(i == 0)
    def _():
        carry_ref[...] = jnp.zeros_like(carry_ref)

    logits = jnp.dot(x_ref[...], wr_ref[...], preferred_element_type=F32, precision=lax.Precision.HIGHEST)
    lt = logits.T[:N_EXPERTS, :]
    ii = lax.broadcasted_iota(jnp.int32, lt.shape, 0)
    m1 = jnp.max(lt, axis=0, keepdims=True)
    i1 = jnp.min(jnp.where(lt == m1, ii, N_EXPERTS), axis=0, keepdims=True)
    l2 = jnp.where(ii == i1, -jnp.inf, lt)
    m2 = jnp.max(l2, axis=0, keepdims=True)
    i2 = jnp.min(jnp.where(l2 == m2, ii, N_EXPERTS), axis=0, keepdims=True)
    e = jnp.exp(m2 - m1)
    gate_ref[0:1, :] = 1.0 / (1.0 + e)
    gate_ref[1:2, :] = e / (1.0 + e)
    idx_ref[0:1, :] = i1
    idx_ref[1:2, :] = i2
    member = (ii == i1) | (ii == i2)
    a = lax.broadcasted_iota(jnp.int32, (tm, tm), 0)
    b = lax.broadcasted_iota(jnp.int32, (tm, tm), 1)
    upper = jnp.where(a < b, 1.0, 0.0).astype(BF16)
    memf = jnp.where(member, 1.0, 0.0)
    rank = jnp.dot(memf.astype(BF16), upper, preferred_element_type=F32) + carry_ref[:, 0:1]
    rank_ref[0:1, :] = jnp.sum(jnp.where(ii == i1, rank, 0.0), axis=0, keepdims=True).astype(jnp.int32)
    rank_ref[1:2, :] = jnp.sum(jnp.where(ii == i2, rank, 0.0), axis=0, keepdims=True).astype(jnp.int32)
    carry_ref[...] = carry_ref[...] + jnp.sum(memf, axis=1, keepdims=True)
    cnt_ref[...] = carry_ref[...].astype(jnp.int32)


def _router(x, wr_pad):
    t = x.shape[0]
    tm = ROUTE_TM
    two = lambda: pl.BlockSpec((2, tm), lambda i: (0, i))
    out_shape = (jax.ShapeDtypeStruct((2, t), jnp.int32), jax.ShapeDtypeStruct((2, t), F32),
                 jax.ShapeDtypeStruct((2, t), jnp.int32), jax.ShapeDtypeStruct((N_EXPERTS, LANES), jnp.int32))
    return pl.pallas_call(
        _router_kernel, out_shape=out_shape, grid=(t // tm,),
        in_specs=[pl.BlockSpec((tm, D_MODEL), lambda i: (i, 0)),
                  pl.BlockSpec((D_MODEL, LANES), lambda i: (0, 0))],
        out_specs=(two(), two(), two(), pl.BlockSpec((N_EXPERTS, LANES), lambda i: (0, 0))),
        scratch_shapes=[pltpu.VMEM((N_EXPERTS, LANES), F32)],
        compiler_params=_params(("arbitrary",)), name="router",
    )(x, wr_pad)


def _row_copy(src_ref, src_row, dst_ref, dst_row, sem):
    return pltpu.make_async_copy(src_ref.at[pl.ds(src_row, 1)], dst_ref.at[pl.ds(dst_row, 1)], sem)


def _scatter_kernel(dest_ref, x_ref, zeros_ref, xbuf_ref, stage_ref, sems):
    del zeros_ref
    tm = x_ref.shape[0]
    i, n = pl.program_id(0), pl.num_programs(0)
    t = n * tm
    slot = i % 2
    stage_ref[slot] = x_ref[...]

    def start(r, carry):
        for k in range(2):
            _row_copy(stage_ref.at[slot], r, xbuf_ref, dest_ref[k * t + i * tm + r], sems.at[slot]).start(priority=k)
        return carry

    def drain(s):
        def wait(r, carry):
            for k in range(2):
                _row_copy(stage_ref.at[s], 0, xbuf_ref, 0, sems.at[s]).wait()
            return carry
        lax.fori_loop(0, tm, wait, 0, unroll=DMA_UNROLL)

    lax.fori_loop(0, tm, start, 0, unroll=DMA_UNROLL)

    @pl.when(i > 0)
    def _():
        drain(1 - slot)

    @pl.when(i == n - 1)
    def _():
        drain(slot)


def _scatter_rows(dest, x, n_rows):
    t = x.shape[0]
    tm = SCATTER_TM
    zeros = jnp.zeros((n_rows, D_MODEL), x.dtype)
    grid_spec = pltpu.PrefetchScalarGridSpec(
        num_scalar_prefetch=1, grid=(t // tm,),
        in_specs=[pl.BlockSpec((tm, D_MODEL), lambda i, d: (i, 0)), pl.BlockSpec(memory_space=pl.ANY)],
        out_specs=pl.BlockSpec(memory_space=pl.ANY),
        scratch_shapes=[pltpu.VMEM((2, tm, D_MODEL), x.dtype), pltpu.SemaphoreType.DMA((2,))],
    )
    return pl.pallas_call(
        _scatter_kernel, out_shape=jax.ShapeDtypeStruct((n_rows, D_MODEL), x.dtype), grid_spec=grid_spec,
        input_output_aliases={2: 0},
        compiler_params=pltpu.CompilerParams(dimension_semantics=("arbitrary",), has_side_effects=True,
                                             disable_bounds_checks=True),
        name="moe_scatter",
    )(dest, x, zeros)


def _expert_kernel(be_ref, nu_ref, x_ref, w1_ref, w3_ref, w2_ref, y_ref, xb_ref, acc_ref):
    i, c = pl.program_id(0), pl.program_id(1)

    @pl.when(i < nu_ref[0])
    def _():
        @pl.when(c == 0)
        def _():
            xb_ref[...] = x_ref[...].astype(BF16)
            acc_ref[...] = jnp.zeros_like(acc_ref)

        for k in range(MOE_FC // MOE_SUB):
            sl = slice(k * MOE_SUB, (k + 1) * MOE_SUB)
            a = _swiglu(xb_ref[...], w1_ref[:, sl], w3_ref[:, sl])
            acc_ref[...] += jnp.dot(a, w2_ref[sl, :], preferred_element_type=F32)

        @pl.when(c == pl.num_programs(1) - 1)
        def _():
            y_ref[...] = acc_ref[...]

    @pl.when(i >= nu_ref[0])
    def _():
        y_ref[...] = jnp.zeros_like(y_ref)


def _expert_ffn(block_expert, n_used, xbuf, w1e, w3e, w2e, j):
    n_rows = xbuf.shape[0]
    bm, fc = MOE_BM, MOE_FC
    nc = D_FF_EXPERT // fc
    row = lambda i, be, nu: jnp.minimum(i, jnp.maximum(nu[0] - 1, 0))
    col = lambda i, c, nu: jnp.where(i < nu[0], c, nc - 1)
    grid_spec = pltpu.PrefetchScalarGridSpec(
        num_scalar_prefetch=2, grid=(n_rows // bm, nc),
        in_specs=[
            pl.BlockSpec((bm, D_MODEL), lambda i, c, be, nu: (row(i, be, nu), 0)),
            pl.BlockSpec((None, None, D_MODEL, fc), lambda i, c, be, nu: (j, be[row(i, be, nu)], 0, col(i, c, nu))),
            pl.BlockSpec((None, None, D_MODEL, fc), lambda i, c, be, nu: (j, be[row(i, be, nu)], 0, col(i, c, nu))),
            pl.BlockSpec((None, None, fc, D_MODEL), lambda i, c, be, nu: (j, be[row(i, be, nu)], col(i, c, nu), 0)),
        ],
        out_specs=pl.BlockSpec((bm, D_MODEL), lambda i, c, be, nu: (i, 0)),
        scratch_shapes=[pltpu.VMEM((bm, D_MODEL), BF16), pltpu.VMEM((bm, D_MODEL), F32)],
    )
    return pl.pallas_call(
        _expert_kernel, out_shape=jax.ShapeDtypeStruct((n_rows, D_MODEL), F32), grid_spec=grid_spec,
        compiler_params=_params(("arbitrary", "arbitrary")), name="expert_ffn",
    )(block_expert, n_used, xbuf, w1e, w3e, w2e)


def _combine_kernel(dest_ref, ybuf_ref, gate_ref, x_ref, lg_ref, lb_ref, x2_ref, x2b_ref, buf_ref, sems):
    tm = x_ref.shape[0]
    i, n = pl.program_id(0), pl.num_programs(0)
    t = n * tm
    slot = i % 2

    def issue(step, s):
        def start(r, carry):
            for k in range(2):
                _row_copy(ybuf_ref, dest_ref[k * t + step * tm + r], buf_ref.at[s, k], r, sems.at[s]).start(priority=k)
            return carry
        lax.fori_loop(0, tm, start, 0, unroll=DMA_UNROLL)

    @pl.when(i == 0)
    def _():
        issue(0, 0)

    @pl.when(i + 1 < n)
    def _():
        issue(i + 1, 1 - slot)

    def wait(r, carry):
        for k in range(2):
            _row_copy(ybuf_ref, 0, buf_ref.at[slot, k], 0, sems.at[slot]).wait()
        return carry

    lax.fori_loop(0, tm, wait, 0, unroll=DMA_UNROLL)
    g = gate_ref[...]
    y = buf_ref[slot, 0] * g[:, 0:1] + buf_ref[slot, 1] * g[:, 1:2]
    x2 = _layer_norm(ALPHA * x_ref[...] + y, lg_ref[...], lb_ref[...])
    x2_ref[...] = x2
    x2b_ref[...] = x2.astype(BF16)


def _combine(dest, ybuf, gates_t, x, ln_g, ln_b, layer):
    t = x.shape[0]
    tm = COMBINE_TM
    row = lambda: pl.BlockSpec((tm, D_MODEL), lambda i, d: (i, 0))
    vec = lambda: pl.BlockSpec((None, 1, D_MODEL), lambda i, d: (layer, 0, 0))
    grid_spec = pltpu.PrefetchScalarGridSpec(
        num_scalar_prefetch=1, grid=(t // tm,),
        in_specs=[pl.BlockSpec(memory_space=pl.ANY), pl.BlockSpec((tm, 2), lambda i, d: (i, 0)), row(), vec(), vec()],
        out_specs=(row(), row()),
        scratch_shapes=[pltpu.VMEM((2, 2, tm, D_MODEL), F32), pltpu.SemaphoreType.DMA((2,))],
    )
    out_shape = (jax.ShapeDtypeStruct((t, D_MODEL), F32), jax.ShapeDtypeStruct((t, D_MODEL), BF16))
    return pl.pallas_call(
        _combine_kernel, out_shape=out_shape, grid_spec=grid_spec,
        compiler_params=pltpu.CompilerParams(dimension_semantics=("arbitrary",), vmem_limit_bytes=VMEM_LIMIT,
                                             disable_bounds_checks=True),
        name="moe_combine_ln",
    )(dest, ybuf, gates_t, x, ln_g, ln_b)


def _moe_ffn(x, wr_pad, w1e, w3e, w2e, ln_g, ln_b, layer, j):
    t = x.shape[0]
    bm = MOE_BM
    idx, gates, rank, cnt = _router(x, wr_pad)
    counts = cnt[:, 0]
    padded = (counts + bm - 1) // bm * bm
    pad_ends = jnp.cumsum(padded)
    pad_starts = pad_ends - padded
    n_rows = -(-(2 * t + N_EXPERTS * (bm - 1)) // bm) * bm
    n_blocks = n_rows // bm
    expert_ids = jnp.arange(N_EXPERTS, dtype=jnp.int32)[:, None, None]
    start_of = jnp.sum(jnp.where(idx[None] == expert_ids, pad_starts[:, None, None], 0), axis=0)
    dest = (start_of + rank).reshape(-1).astype(jnp.int32)
    block_start = jnp.arange(n_blocks, dtype=jnp.int32) * bm
    block_expert = jnp.minimum(jnp.sum(block_start[:, None] >= pad_ends[None, :], axis=1), N_EXPERTS - 1)
    n_used = (pad_ends[-1:] // bm).astype(jnp.int32)
    xbuf = _scatter_rows(dest, x, n_rows)
    ybuf = _expert_ffn(block_expert.astype(jnp.int32), n_used, xbuf, w1e, w3e, w2e, j)
    return _combine(dest, ybuf, gates.T, x, ln_g, ln_b, layer)


def _rotary_tables(seq):
    inv_freq = ROPE_THETA ** (-jnp.arange(0, HEAD_DIM, 2, dtype=F32) / HEAD_DIM)
    ang = jnp.arange(seq, dtype=F32)[:, None] * inv_freq[None, :]
    cos, sin = jnp.cos(ang), jnp.sin(ang)
    cos_t, sin_t = jnp.tile(cos, (1, 4)), jnp.concatenate([-sin, -sin, sin, sin], axis=1)
    q_scale = HEAD_DIM ** -0.5 * math.log2(math.e)
    return jnp.concatenate([cos_t * q_scale, sin_t * q_scale, cos_t, sin_t], axis=1)


_B_HEAD_ORDER = tuple(h for j in range(N_HEADS_B // N_KV_B) for h in (j, N_HEADS_B // N_KV_B + j))


def kernel(x, w_in, b_gate, sink_logits, w_br_a, w_br_b, w_out, ln_mix_g, ln_mix_b, w1_dense, w3_dense,
           w2_dense, w_router, w1_exp, w3_exp, w2_exp, ln_ffn_g, ln_ffn_b):
    b, s, d = x.shape
    assert b == 1 and d == D_MODEL and s % (16 * 2 * ATT_QS) == 0 and s % IN_TM == 0
    rope_t = _rotary_tables(s)
    order = jnp.asarray(_B_HEAD_ORDER)

    def rotary_layout(w, head_order=None):
        heads = w.shape[-1] // HEAD_DIM
        w = w.reshape(DEPTH, D_MODEL, heads, 2, HEAD_DIM // 2)
        if head_order is not None:
            w = w[:, :, head_order]
        w = w.reshape(DEPTH, D_MODEL, heads // 2, 2, 2, HEAD_DIM // 2).transpose(0, 1, 2, 4, 3, 5)
        return w.reshape(DEPTH, D_MODEL, heads * HEAD_DIM)

    kb0 = QKV_A + WIDTH_B
    w_in_b = jnp.concatenate([
        rotary_layout(w_in[:, :, :WIDTH_A]), rotary_layout(w_in[:, :, WIDTH_A:2 * WIDTH_A]),
        w_in[:, :, 2 * WIDTH_A:QKV_A], rotary_layout(w_in[:, :, QKV_A:kb0], order),
        rotary_layout(w_in[:, :, kb0:kb0 + WIDTH_KV_B]), w_in[:, :, kb0 + WIDTH_KV_B:]], axis=2).astype(BF16)
    w_br_b_b = w_br_b.reshape(DEPTH, N_HEADS_B, HEAD_DIM, D_MODEL)[:, order].reshape(w_br_b.shape).astype(BF16)
    w_br_a_b, w_out_b = w_br_a.astype(BF16), w_out.astype(BF16)
    n_dense, nc = w1_dense.shape[0], D_FF_DENSE // DENSE_FC
    chunk_cols = lambda w: w.astype(BF16).reshape(n_dense, D_MODEL, nc, DENSE_FC).transpose(0, 2, 1, 3)
    w1d, w3d = chunk_cols(w1_dense), chunk_cols(w3_dense)
    w2d = w2_dense.astype(BF16).reshape(n_dense, nc, DENSE_FC, D_MODEL)
    w1e, w3e, w2e = w1_exp.astype(BF16), w3_exp.astype(BF16), w2_exp.astype(BF16)
    wr_pad = jnp.pad(w_router, ((0, 0), (0, 0), (0, LANES - N_EXPERTS)))
    vec3 = lambda v: v.reshape(v.shape[0], 1, v.shape[1])
    b_gate3, lmg, lmb, lfg, lfb = map(vec3, (b_gate, ln_mix_g, ln_mix_b, ln_ffn_g, ln_ffn_b))

    xf = x.reshape(s, d)
    xb = xf.astype(BF16)
    for i in range(DEPTH):
        qkv1, qkv4, qkv16, qb, kvb, g = _inproj(xb, w_in_b, i, rope_t)
        o1, l1 = _attn_a(qkv1.reshape(1, s, QKV_A))
        o4, l4 = _attn_a(qkv4)
        o16, l16 = _attn_a(qkv16)
        yb = _attn_b(qb, kvb, sink_logits[i])
        xf, xb = _merge(o1.reshape(s, WIDTH_A), l1.reshape(s, WIDTH_A), o4, l4, o16, l16, yb, g, xf,
                        b_gate3, w_br_a_b, w_br_b_b, w_out_b, lmg, lmb, i)
        j = i // 2
        if i % 2 == 0:
            xf, xb = _dense_ffn(xb, xf, w1d, w3d, w2d, lfg, lfb, i, j)
        else:
            xf, xb = _moe_ffn(xf, wr_pad[j], w1e, w3e, w2e, lfg, lfb, i, j)
    return xf.reshape(b, s, d)
```

```python
import functools
import math

import jax
import jax.numpy as jnp
from jax import lax
from jax.experimental import pallas as pl
from jax.experimental.pallas import tpu as pltpu

F32 = jnp.float32
BF16 = jnp.bfloat16

D_MODEL = 1024
DEPTH = 4
HEAD_DIM = 64
N_HEADS_A = 8
DILATIONS = (1, 4, 16)
HALF_W_A = 64
N_HEADS_B = 8
N_KV_B = 2
HALF_W_B = 128
ROPE_THETA = 10000.0
WIDTH_A = N_HEADS_A * HEAD_DIM
WIDTH_B = N_HEADS_B * HEAD_DIM
WIDTH_KV_B = N_KV_B * HEAD_DIM
QKV_A = 3 * WIDTH_A
IN_COLS = QKV_A + WIDTH_B + 2 * WIDTH_KV_B + 2 * D_MODEL
D_FF_DENSE = 2816
N_EXPERTS = 8
D_FF_EXPERT = 3584
ALPHA = (2 * DEPTH) ** 0.25
LN_EPS = 1e-5
NEG_INF = -1e30

LANES = 128
VMEM_LIMIT = 56 * 1024 * 1024

IN_TM, IN_TN = 512, 256
ATT_QS = 128
ATT_A_UNROLL, ATT_B_UNROLL = 8, 2
MERGE_TM = 512
DENSE_TM, DENSE_FC = 1024, 256
ROUTE_TM = 512
MOE_BM = 512
MOE_FC = 1792
MOE_SUB = 256
SCATTER_TM = 512
COMBINE_TM = 512
DMA_UNROLL = 8


def _params(sem):
    return pltpu.CompilerParams(dimension_semantics=sem, vmem_limit_bytes=VMEM_LIMIT)


def _layer_norm(z, g, b):
    mu = jnp.mean(z, axis=-1, keepdims=True)
    zc = z - mu
    var = jnp.mean(zc * zc, axis=-1, keepdims=True)
    return zc * lax.rsqrt(var + LN_EPS) * g + b


def _rope(a, cos, sin):
    return a * cos + pltpu.roll(a, LANES // 2, axis=1) * sin


def _inproj_kernel(x_ref, w_ref, rope_ref, qkv1_ref, qkv4_ref, qkv16_ref, qb_ref, kvb_ref, g_ref, rs_ref):
    tm = x_ref.shape[0]
    x = x_ref[...]
    rope_q = lambda v: _rope(v, rope_ref[:, 0:LANES], rope_ref[:, LANES:2 * LANES])
    rope_k = lambda v: _rope(v, rope_ref[:, 2 * LANES:3 * LANES], rope_ref[:, 3 * LANES:])
    plain = lambda v: v
    n_q, n_a = WIDTH_A // LANES, QKV_A // LANES
    qb0 = n_a
    kb0 = qb0 + WIDTH_B // LANES
    vb0 = kb0 + WIDTH_KV_B // LANES
    g0 = vb0 + WIDTH_KV_B // LANES
    per_dot = IN_TN // LANES

    for t in range(IN_COLS // IN_TN):
        acc = jnp.dot(x, w_ref[:, t * IN_TN:(t + 1) * IN_TN], preferred_element_type=F32)
        for h in range(per_dot):
            c = t * per_dot + h
            blk = acc[:, h * LANES:(h + 1) * LANES]
            if c < n_a:
                post = rope_q if c < n_q else rope_k if c < 2 * n_q else plain
                sl = slice(c * LANES, (c + 1) * LANES)
                rs_ref[c] = post(blk)
                qkv1_ref[:, sl] = rs_ref[c].astype(BF16)
                for r, ref in ((4, qkv4_ref), (16, qkv16_ref)):
                    for res in range(r):
                        ref[res, :, sl] = rs_ref[c, pl.ds(res, tm // r, stride=r), :].astype(BF16)
            elif c < kb0:
                qb_ref[:, (c - qb0) * LANES:(c - qb0 + 1) * LANES] = rope_q(blk).astype(BF16)
            elif c < vb0:
                kvb_ref[:, :LANES] = rope_k(blk).astype(BF16)
            elif c < g0:
                kvb_ref[:, LANES:] = blk.astype(BF16)
            else:
                g_ref[:, (c - g0) * LANES:(c - g0 + 1) * LANES] = blk.astype(BF16)


def _inproj(xb, w_in_b, layer, rope_t):
    s = xb.shape[0]
    tm = IN_TM
    out_shape = (
        jax.ShapeDtypeStruct((s, QKV_A), BF16),
        jax.ShapeDtypeStruct((4, s // 4, QKV_A), BF16),
        jax.ShapeDtypeStruct((16, s // 16, QKV_A), BF16),
        jax.ShapeDtypeStruct((s, WIDTH_B), BF16),
        jax.ShapeDtypeStruct((s, 2 * WIDTH_KV_B), BF16),
        jax.ShapeDtypeStruct((s, 2 * D_MODEL), BF16),
    )
    row = lambda w: pl.BlockSpec((tm, w), lambda i: (i, 0))
    out_specs = (
        row(QKV_A),
        pl.BlockSpec((4, tm // 4, QKV_A), lambda i: (0, i, 0)),
        pl.BlockSpec((16, tm // 16, QKV_A), lambda i: (0, i, 0)),
        row(WIDTH_B), row(2 * WIDTH_KV_B), row(2 * D_MODEL),
    )
    in_specs = [
        row(D_MODEL),
        pl.BlockSpec((None, D_MODEL, IN_COLS), lambda i: (layer, 0, 0), pipeline_mode=pl.Buffered(1)),
        row(4 * LANES),
    ]
    return pl.pallas_call(
        _inproj_kernel, out_shape=out_shape, grid=(s // tm,), in_specs=in_specs, out_specs=out_specs,
        scratch_shapes=[pltpu.VMEM((QKV_A // LANES, tm, LANES), F32)],
        compiler_params=_params(("arbitrary",)), name="inproj",
    )(xb, w_in_b, rope_t)


def _pair_attention(q_pairs, k_pairs, v_pairs, valid, sinks=None):
    qs = q_pairs[0].shape[0]
    lane = lax.broadcasted_iota(jnp.int32, (qs, LANES), 1)
    lo = lane < HEAD_DIM
    q_first = (lane % HEAD_DIM) < HEAD_DIM // 2
    nt = (((1,), (1,)), ((), ()))
    scores = []
    for qp, kp in zip(q_pairs, k_pairs):
        for half in (q_first, ~q_first):
            scores.append(lax.dot_general(jnp.where(half, qp, jnp.zeros_like(qp)), kp, nt,
                                          preferred_element_type=F32))
    stats = []
    for idx, s in enumerate(scores):
        s = jnp.where(valid, s, NEG_INF)
        m = jnp.max(s, axis=-1, keepdims=True)
        if sinks is not None:
            m = jnp.maximum(m, sinks[idx])
        p = jnp.exp2(s - m)
        d = jnp.sum(p, axis=-1, keepdims=True)
        if sinks is not None:
            d = d + jnp.exp2(sinks[idx] - m)
        stats.append((p.astype(BF16), m, d))
    outs, lses = [], []
    for j, vp in enumerate(v_pairs):
        (p0, m0, d0), (p1, m1, d1) = stats[2 * j], stats[2 * j + 1]
        o0 = jnp.dot(p0, vp, preferred_element_type=F32)
        o1 = jnp.dot(p1, vp, preferred_element_type=F32)
        d = jnp.where(lo, d0, d1)
        outs.append(jnp.where(lo, o0, o1) / d)
        lses.append(jnp.where(lo, m0, m1) + jnp.log2(d))
    return outs, lses


def _fill_slab(slab_ref, prev_ref, main_ref, next_ref):
    w, bq = prev_ref.shape[0], main_ref.shape[0]
    slab_ref[0:w] = prev_ref[...]
    slab_ref[w:w + bq] = main_ref[...]
    slab_ref[w + bq:] = next_ref[...]


def _band_mask(block_row0, off, n, half_w, qs):
    kw = qs + 2 * half_w
    row = lax.broadcasted_iota(jnp.int32, (qs, kw), 0)
    col = lax.broadcasted_iota(jnp.int32, (qs, kw), 1)
    kpos = block_row0 + off - half_w + col
    return (jnp.abs(col - half_w - row) <= half_w) & (kpos >= 0) & (kpos < n)


def _attn_a_kernel(q_ref, km_ref, kp_ref, kn_ref, vm_ref, vp_ref, vn_ref, o_ref, l_ref, ks_ref, vs_ref, *, n):
    i = pl.program_id(1)
    bq = q_ref.shape[0]
    qs, w = ATT_QS, HALF_W_A
    kw = qs + 2 * w
    n_pair = WIDTH_A // LANES
    _fill_slab(ks_ref, kp_ref, km_ref, kn_ref)
    _fill_slab(vs_ref, vp_ref, vm_ref, vn_ref)

    def body(sb, carry):
        off = pl.multiple_of(sb * qs, qs)
        valid = _band_mask(i * bq, off, n, w, qs)
        q = q_ref[pl.ds(off, qs), :]
        kwin = ks_ref[pl.ds(off, kw), :]
        vwin = vs_ref[pl.ds(off, kw), :]
        pairs = [slice(j * LANES, (j + 1) * LANES) for j in range(n_pair)]
        outs, lses = _pair_attention([q[:, sl] for sl in pairs], [kwin[:, sl] for sl in pairs],
                                     [vwin[:, sl] for sl in pairs], valid)
        for sl, o, lse in zip(pairs, outs, lses):
            o_ref[pl.ds(off, qs), sl] = o.astype(BF16)
            l_ref[pl.ds(off, qs), sl] = lse
        return carry

    lax.fori_loop(0, bq // qs, body, 0, unroll=ATT_A_UNROLL)


def _halo_specs(bq, w, n, width, col):
    nb = bq // w
    main = pl.BlockSpec((None, bq, width), lambda c, i: (c, i, col))
    prev = pl.BlockSpec((None, w, width), lambda c, i: (c, jnp.maximum(i * nb - 1, 0), col))
    nxt = pl.BlockSpec((None, w, width), lambda c, i: (c, jnp.minimum((i + 1) * nb, n // w - 1), col))
    return [main, prev, nxt]


def _attn_a(qkv_r):
    r, n, _ = qkv_r.shape
    bq = min(n, 1024)
    w = HALF_W_A
    blk = lambda: pl.BlockSpec((None, bq, WIDTH_A), lambda c, i: (c, i, 0))
    in_specs = [blk()] + _halo_specs(bq, w, n, WIDTH_A, 1) + _halo_specs(bq, w, n, WIDTH_A, 2)
    out_shape = (jax.ShapeDtypeStruct((r, n, WIDTH_A), BF16), jax.ShapeDtypeStruct((r, n, WIDTH_A), F32))
    return pl.pallas_call(
        functools.partial(_attn_a_kernel, n=n), out_shape=out_shape, grid=(r, n // bq),
        in_specs=in_specs, out_specs=(blk(), blk()),
        scratch_shapes=[pltpu.VMEM((bq + 2 * w, WIDTH_A), BF16)] * 2,
        compiler_params=_params(("arbitrary",) * 2), name=f"attn_a_r{r}",
    )(*([qkv_r] * 7))


def _attn_b_kernel(sink_ref, q_ref, kvm_ref, kvp_ref, kvn_ref, o_ref, slab_ref, *, n):
    i = pl.program_id(1)
    bq = q_ref.shape[0]
    qs, w = ATT_QS, HALF_W_B
    kw = qs + 2 * w
    n_pair = WIDTH_B // LANES
    _fill_slab(slab_ref, kvp_ref, kvm_ref, kvn_ref)
    log2e = math.log2(math.e)
    sinks = [sink_ref[hh * n_pair + j] * log2e for j in range(n_pair) for hh in range(2)]

    def body(sb, carry):
        off = pl.multiple_of(sb * qs, qs)
        valid = _band_mask(i * bq, off, n, w, qs)
        q = q_ref[pl.ds(off, qs), :]
        win = slab_ref[pl.ds(off, kw), :]
        kb, vb = win[:, :LANES], win[:, LANES:]
        pairs = [slice(j * LANES, (j + 1) * LANES) for j in range(n_pair)]
        outs, _ = _pair_attention([q[:, sl] for sl in pairs], [kb] * n_pair, [vb] * n_pair, valid, sinks)
        for sl, o in zip(pairs, outs):
            o_ref[pl.ds(off, qs), sl] = o.astype(BF16)
        return carry

    lax.fori_loop(0, bq // qs, body, 0, unroll=ATT_B_UNROLL)


def _attn_b(qb, kvb, sink):
    s = qb.shape[0]
    bq = min(s, 1024)
    w = HALF_W_B
    kv_w = 2 * WIDTH_KV_B
    drop = lambda spec: pl.BlockSpec(spec.block_shape, lambda c, i, sk, f=spec.index_map: f(c, i))
    grid_spec = pltpu.PrefetchScalarGridSpec(
        num_scalar_prefetch=1, grid=(1, s // bq),
        in_specs=[pl.BlockSpec((None, bq, WIDTH_B), lambda c, i, sk: (c, i, 0))]
        + [drop(sp) for sp in _halo_specs(bq, w, s, kv_w, 0)],
        out_specs=pl.BlockSpec((None, bq, WIDTH_B), lambda c, i, sk: (c, i, 0)),
        scratch_shapes=[pltpu.VMEM((bq + 2 * w, kv_w), BF16)],
    )
    kvb3 = kvb.reshape(1, s, kv_w)
    out = pl.pallas_call(
        functools.partial(_attn_b_kernel, n=s), out_shape=jax.ShapeDtypeStruct((1, s, WIDTH_B), BF16),
        grid_spec=grid_spec, compiler_params=_params(("arbitrary",) * 2), name="attn_b",
    )(sink, qb.reshape(1, s, WIDTH_B), kvb3, kvb3, kvb3)
    return out.reshape(s, WIDTH_B)


def _merge_kernel(o1_ref, l1_ref, o4_ref, l4_ref, o16_ref, l16_ref, yb_ref, g_ref, x_ref, bg_ref,
                  wa_ref, wb_ref, wo_ref, lg_ref, lb_ref, x1_ref, x1b_ref, so4, sl4, so16, sl16):
    tm = x_ref.shape[0]
    for r, src_o, src_l, dst_o, dst_l in ((4, o4_ref, l4_ref, so4, sl4), (16, o16_ref, l16_ref, so16, sl16)):
        for c in range(r):
            for h in range(WIDTH_A // LANES):
                sl = slice(h * LANES, (h + 1) * LANES)
                dst_o[h, pl.ds(c, tm // r, stride=r), :] = src_o[c, :, sl].astype(F32)
                dst_l[h, pl.ds(c, tm // r, stride=r), :] = src_l[c, :, sl]
    slabs = lambda ref: jnp.concatenate([ref[h] for h in range(WIDTH_A // LANES)], axis=1)
    l1, l4, l16 = l1_ref[...], slabs(sl4), slabs(sl16)
    lmax = jnp.maximum(jnp.maximum(l1, l4), l16)
    e1, e4, e16 = jnp.exp2(l1 - lmax), jnp.exp2(l4 - lmax), jnp.exp2(l16 - lmax)
    y_a = (e1 * o1_ref[...].astype(F32) + e4 * slabs(so4) + e16 * slabs(so16)) / (e1 + e4 + e16)
    ta = jnp.dot(y_a.astype(BF16), wa_ref[...], preferred_element_type=F32)
    tb = jnp.dot(yb_ref[...], wb_ref[...], preferred_element_type=F32)
    gates = jax.nn.sigmoid(g_ref[...].astype(F32) + bg_ref[...])
    merged = gates[:, :D_MODEL] * ta + gates[:, D_MODEL:] * tb
    mix = jnp.dot(merged.astype(BF16), wo_ref[...], preferred_element_type=F32)
    x1 = _layer_norm(ALPHA * x_ref[...] + mix, lg_ref[...], lb_ref[...])
    x1_ref[...] = x1
    x1b_ref[...] = x1.astype(BF16)


def _merge(o1, l1, o4, l4, o16, l16, yb, g, x, b_gate, w_br_a_b, w_br_b_b, w_out_b, ln_g, ln_b, layer):
    s = x.shape[0]
    tm = MERGE_TM
    row = lambda w: pl.BlockSpec((tm, w), lambda i: (i, 0))
    perm = lambda r: pl.BlockSpec((r, tm // r, WIDTH_A), lambda i: (0, i, 0))
    vec = lambda w: pl.BlockSpec((None, 1, w), lambda i: (layer, 0, 0))
    wsp = lambda k, n: pl.BlockSpec((None, k, n), lambda i: (layer, 0, 0))
    in_specs = [row(WIDTH_A), row(WIDTH_A), perm(4), perm(4), perm(16), perm(16), row(WIDTH_B),
                row(2 * D_MODEL), row(D_MODEL), vec(2 * D_MODEL),
                wsp(WIDTH_A, D_MODEL), wsp(WIDTH_B, D_MODEL), wsp(D_MODEL, D_MODEL),
                vec(D_MODEL), vec(D_MODEL)]
    out_shape = (jax.ShapeDtypeStruct((s, D_MODEL), F32), jax.ShapeDtypeStruct((s, D_MODEL), BF16))
    return pl.pallas_call(
        _merge_kernel, out_shape=out_shape, grid=(s // tm,), in_specs=in_specs,
        out_specs=(row(D_MODEL), row(D_MODEL)),
        scratch_shapes=[pltpu.VMEM((WIDTH_A // LANES, tm, LANES), F32)] * 4,
        compiler_params=_params(("arbitrary",)), name="merge_outproj_ln",
    )(o1, l1, o4, l4, o16, l16, yb, g, x, b_gate, w_br_a_b, w_br_b_b, w_out_b, ln_g, ln_b)


def _swiglu(xb, w1, w3):
    h = jnp.dot(xb, w1, preferred_element_type=F32)
    u = jnp.dot(xb, w3, preferred_element_type=F32)
    return (h * jax.nn.sigmoid(h) * u).astype(BF16)


def _dense_kernel(xb_ref, x_ref, w1_ref, w3_ref, w2_ref, lg_ref, lb_ref, x2_ref, x2b_ref, acc_ref):
    n_chunks = w1_ref.shape[0]
    acc_ref[...] = jnp.zeros_like(acc_ref)

    def body(c, carry):
        a = _swiglu(xb_ref[...], w1_ref[c], w3_ref[c])
        acc_ref[...] += jnp.dot(a, w2_ref[c], preferred_element_type=F32)
        return carry

    lax.fori_loop(0, n_chunks, body, 0)
    x2 = _layer_norm(ALPHA * x_ref[...] + acc_ref[...], lg_ref[...], lb_ref[...])
    x2_ref[...] = x2
    x2b_ref[...] = x2.astype(BF16)


def _dense_ffn(xb, x, w1d, w3d, w2d, ln_g, ln_b, layer, j):
    s = x.shape[0]
    tm, fc = DENSE_TM, DENSE_FC
    nc = D_FF_DENSE // fc
    row = lambda: pl.BlockSpec((tm, D_MODEL), lambda i: (i, 0))
    vec = lambda: pl.BlockSpec((None, 1, D_MODEL), lambda i: (layer, 0, 0))
    resident = dict(pipeline_mode=pl.Buffered(1))
    in_specs = [row(), row(),
                pl.BlockSpec((None, nc, D_MODEL, fc), lambda i: (j, 0, 0, 0), **resident),
                pl.BlockSpec((None, nc, D_MODEL, fc), lambda i: (j, 0, 0, 0), **resident),
                pl.BlockSpec((None, nc, fc, D_MODEL), lambda i: (j, 0, 0, 0), **resident),
                vec(), vec()]
    out_shape = (jax.ShapeDtypeStruct((s, D_MODEL), F32), jax.ShapeDtypeStruct((s, D_MODEL), BF16))
    return pl.pallas_call(
        _dense_kernel, out_shape=out_shape, grid=(s // tm,), in_specs=in_specs,
        out_specs=(row(), row()), scratch_shapes=[pltpu.VMEM((tm, D_MODEL), F32)],
        compiler_params=_params(("arbitrary",)), name="dense_ffn_ln",
    )(xb, x, w1d, w3d, w2d, ln_g, ln_b)


def _router_kernel(x_ref, wh_ref, wl_ref, idx_ref, gate_ref, rank_ref, cnt_ref, carry_ref):
    i = pl.program_id(0)
    tm = x_ref.shape[0]

    @pl.when(i == 0)
    def _():
        carry_ref[...] = jnp.zeros_like(carry_ref)

    x = x_ref[...]
    xh = x.astype(BF16)
    xl = (x - xh.astype(F32)).astype(BF16)
    logits = (jnp.dot(xh, wh_ref[...], preferred_element_type=F32)
              + jnp.dot(xl, wh_ref[...], preferred_element_type=F32)
              + jnp.dot(xh, wl_ref[...], preferred_element_type=F32))
    lt = logits.T[:N_EXPERTS, :]
    ii = lax.broadcasted_iota(jnp.int32, lt.shape, 0)
    m1 = jnp.max(lt, axis=0, keepdims=True)
    i1 = jnp.min(jnp.where(lt == m1, ii, N_EXPERTS), axis=0, keepdims=True)
    l2 = jnp.where(ii == i1, -jnp.inf, lt)
    m2 = jnp.max(l2, axis=0, keepdims=True)
    i2 = jnp.min(jnp.where(l2 == m2, ii, N_EXPERTS), axis=0, keepdims=True)
    e = jnp.exp(m2 - m1)
    gate_ref[0:1, :] = 1.0 / (1.0 + e)
    gate_ref[1:2, :] = e / (1.0 + e)
    idx_ref[0:1, :] = i1
    idx_ref[1:2, :] = i2
    member = (ii == i1) | (ii == i2)
    a = lax.broadcasted_iota(jnp.int32, (tm, tm), 0)
    b = lax.broadcasted_iota(jnp.int32, (tm, tm), 1)
    upper = jnp.where(a < b, 1.0, 0.0).astype(BF16)
    memf = jnp.where(member, 1.0, 0.0)
    rank = jnp.dot(memf.astype(BF16), upper, preferred_element_type=F32) + carry_ref[:, 0:1]
    rank_ref[0:1, :] = jnp.sum(jnp.where(ii == i1, rank, 0.0), axis=0, keepdims=True).astype(jnp.int32)
    rank_ref[1:2, :] = jnp.sum(jnp.where(ii == i2, rank, 0.0), axis=0, keepdims=True).astype(jnp.int32)
    carry_ref[...] = carry_ref[...] + jnp.sum(memf, axis=1, keepdims=True)
    cnt_ref[...] = carry_ref[...].astype(jnp.int32)


def _router(x, wr_hi, wr_lo):
    t = x.shape[0]
    tm = ROUTE_TM
    two = lambda: pl.BlockSpec((2, tm), lambda i: (0, i))
    wsp = lambda: pl.BlockSpec((D_MODEL, LANES), lambda i: (0, 0))
    out_shape = (jax.ShapeDtypeStruct((2, t), jnp.int32), jax.ShapeDtypeStruct((2, t), F32),
                 jax.ShapeDtypeStruct((2, t), jnp.int32), jax.ShapeDtypeStruct((N_EXPERTS, LANES), jnp.int32))
    return pl.pallas_call(
        _router_kernel, out_shape=out_shape, grid=(t // tm,),
        in_specs=[pl.BlockSpec((tm, D_MODEL), lambda i: (i, 0)), wsp(), wsp()],
        out_specs=(two(), two(), two(), pl.BlockSpec((N_EXPERTS, LANES), lambda i: (0, 0))),
        scratch_shapes=[pltpu.VMEM((N_EXPERTS, LANES), F32)],
        compiler_params=_params(("arbitrary",)), name="router",
    )(x, wr_hi, wr_lo)


def _row_copy(src_ref, src_row, dst_ref, dst_row, sem):
    return pltpu.make_async_copy(src_ref.at[pl.ds(src_row, 1)], dst_ref.at[pl.ds(dst_row, 1)], sem)


def _scatter_kernel(dest_ref, x_ref, zeros_ref, xbuf_ref, stage_ref, sems):
    del zeros_ref
    tm = x_ref.shape[0]
    i, n = pl.program_id(0), pl.num_programs(0)
    t = n * tm
    slot = i % 2
    stage_ref[slot] = x_ref[...]

    def start(r, carry):
        for k in range(2):
            _row_copy(stage_ref.at[slot], r, xbuf_ref, dest_ref[k * t + i * tm + r], sems.at[slot]).start(priority=k)
        return carry

    def drain(s):
        def wait(r, carry):
            for k in range(2):
                _row_copy(stage_ref.at[s], 0, xbuf_ref, 0, sems.at[s]).wait()
            return carry
        lax.fori_loop(0, tm, wait, 0, unroll=DMA_UNROLL)

    lax.fori_loop(0, tm, start, 0, unroll=DMA_UNROLL)

    @pl.when(i > 0)
    def _():
        drain(1 - slot)

    @pl.when(i == n - 1)
    def _():
        drain(slot)


def _scatter_rows(dest, x, n_rows, init=None):
    t = x.shape[0]
    tm = SCATTER_TM
    zeros = jnp.zeros((n_rows, D_MODEL), x.dtype) if init is None else init
    grid_spec = pltpu.PrefetchScalarGridSpec(
        num_scalar_prefetch=1, grid=(t // tm,),
        in_specs=[pl.BlockSpec((tm, D_MODEL), lambda i, d: (i, 0)), pl.BlockSpec(memory_space=pl.ANY)],
        out_specs=pl.BlockSpec(memory_space=pl.ANY),
        scratch_shapes=[pltpu.VMEM((2, tm, D_MODEL), x.dtype), pltpu.SemaphoreType.DMA((2,))],
    )
    return pl.pallas_call(
        _scatter_kernel, out_shape=jax.ShapeDtypeStruct((n_rows, D_MODEL), x.dtype), grid_spec=grid_spec,
        input_output_aliases={2: 0},
        compiler_params=pltpu.CompilerParams(dimension_semantics=("arbitrary",), has_side_effects=True,
                                             disable_bounds_checks=True),
        name="moe_scatter",
    )(dest, x, zeros)


def _expert_kernel(be_ref, nu_ref, x_ref, w1_ref, w3_ref, w2_ref, y_ref, xb_ref, acc_ref):
    i, c = pl.program_id(0), pl.program_id(1)

    @pl.when(i < nu_ref[0])
    def _():
        @pl.when(c == 0)
        def _():
            xb_ref[...] = x_ref[...].astype(BF16)
            acc_ref[...] = jnp.zeros_like(acc_ref)

        for k in range(MOE_FC // MOE_SUB):
            sl = slice(k * MOE_SUB, (k + 1) * MOE_SUB)
            a = _swiglu(xb_ref[...], w1_ref[:, sl], w3_ref[:, sl])
            acc_ref[...] += jnp.dot(a, w2_ref[sl, :], preferred_element_type=F32)

        @pl.when(c == pl.num_programs(1) - 1)
        def _():
            y_ref[...] = acc_ref[...]

    @pl.when(i >= nu_ref[0])
    def _():
        y_ref[...] = jnp.zeros_like(y_ref)


def _expert_ffn(block_expert, n_used, xbuf, w1e, w3e, w2e, j):
    n_rows = xbuf.shape[0]
    bm, fc = MOE_BM, MOE_FC
    nc = D_FF_EXPERT // fc
    row = lambda i, be, nu: jnp.minimum(i, jnp.maximum(nu[0] - 1, 0))
    col = lambda i, c, nu: jnp.where(i < nu[0], c, nc - 1)
    grid_spec = pltpu.PrefetchScalarGridSpec(
        num_scalar_prefetch=2, grid=(n_rows // bm, nc),
        in_specs=[
            pl.BlockSpec((bm, D_MODEL), lambda i, c, be, nu: (row(i, be, nu), 0)),
            pl.BlockSpec((None, None, D_MODEL, fc), lambda i, c, be, nu: (j, be[row(i, be, nu)], 0, col(i, c, nu))),
            pl.BlockSpec((None, None, D_MODEL, fc), lambda i, c, be, nu: (j, be[row(i, be, nu)], 0, col(i, c, nu))),
            pl.BlockSpec((None, None, fc, D_MODEL), lambda i, c, be, nu: (j, be[row(i, be, nu)], col(i, c, nu), 0)),
        ],
        out_specs=pl.BlockSpec((bm, D_MODEL), lambda i, c, be, nu: (i, 0)),
        scratch_shapes=[pltpu.VMEM((bm, D_MODEL), BF16), pltpu.VMEM((bm, D_MODEL), F32)],
    )
    return pl.pallas_call(
        _expert_kernel, out_shape=jax.ShapeDtypeStruct((n_rows, D_MODEL), F32), grid_spec=grid_spec,
        compiler_params=_params(("arbitrary", "arbitrary")), name="expert_ffn",
    )(block_expert, n_used, xbuf, w1e, w3e, w2e)


def _combine_kernel(dest_ref, ybuf_ref, gate_ref, x_ref, lg_ref, lb_ref, x2_ref, x2b_ref, buf_ref, sems):
    tm = x_ref.shape[0]
    i, n = pl.program_id(0), pl.num_programs(0)
    t = n * tm
    slot = i % 2

    def issue(step, s):
        def start(r, carry):
            for k in range(2):
                _row_copy(ybuf_ref, dest_ref[k * t + step * tm + r], buf_ref.at[s, k], r, sems.at[s]).start(priority=k)
            return carry
        lax.fori_loop(0, tm, start, 0, unroll=DMA_UNROLL)

    @pl.when(i == 0)
    def _():
        issue(0, 0)

    @pl.when(i + 1 < n)
    def _():
        issue(i + 1, 1 - slot)

    def wait(r, carry):
        for k in range(2):
            _row_copy(ybuf_ref, 0, buf_ref.at[slot, k], 0, sems.at[slot]).wait()
        return carry

    lax.fori_loop(0, tm, wait, 0, unroll=DMA_UNROLL)
    g = gate_ref[...]
    y = buf_ref[slot, 0] * g[:, 0:1] + buf_ref[slot, 1] * g[:, 1:2]
    x2 = _layer_norm(ALPHA * x_ref[...] + y, lg_ref[...], lb_ref[...])
    x2_ref[...] = x2
    x2b_ref[...] = x2.astype(BF16)


def _combine(dest, ybuf, gates_t, x, ln_g, ln_b, layer):
    t = x.shape[0]
    tm = COMBINE_TM
    row = lambda: pl.BlockSpec((tm, D_MODEL), lambda i, d: (i, 0))
    vec = lambda: pl.BlockSpec((None, 1, D_MODEL), lambda i, d: (layer, 0, 0))
    grid_spec = pltpu.PrefetchScalarGridSpec(
        num_scalar_prefetch=1, grid=(t // tm,),
        in_specs=[pl.BlockSpec(memory_space=pl.ANY), pl.BlockSpec((tm, 2), lambda i, d: (i, 0)), row(), vec(), vec()],
        out_specs=(row(), row()),
        scratch_shapes=[pltpu.VMEM((2, 2, tm, D_MODEL), F32), pltpu.SemaphoreType.DMA((2,))],
    )
    out_shape = (jax.ShapeDtypeStruct((t, D_MODEL), F32), jax.ShapeDtypeStruct((t, D_MODEL), BF16))
    return pl.pallas_call(
        _combine_kernel, out_shape=out_shape, grid_spec=grid_spec,
        compiler_params=pltpu.CompilerParams(dimension_semantics=("arbitrary",), vmem_limit_bytes=VMEM_LIMIT,
                                             disable_bounds_checks=True),
        name="moe_combine_ln",
    )(dest, ybuf, gates_t, x, ln_g, ln_b)


def _moe_ffn(x, wr_hi, wr_lo, w1e, w3e, w2e, ln_g, ln_b, layer, j, xbuf_init=None):
    t = x.shape[0]
    bm = MOE_BM
    idx, gates, rank, cnt = _router(x, wr_hi, wr_lo)
    counts = cnt[:, 0]
    padded = (counts + bm - 1) // bm * bm
    pad_ends = jnp.cumsum(padded)
    pad_starts = pad_ends - padded
    n_rows = -(-(2 * t + N_EXPERTS * (bm - 1)) // bm) * bm
    n_blocks = n_rows // bm
    expert_ids = jnp.arange(N_EXPERTS, dtype=jnp.int32)[:, None, None]
    start_of = jnp.sum(jnp.where(idx[None] == expert_ids, pad_starts[:, None, None], 0), axis=0)
    dest = (start_of + rank).reshape(-1).astype(jnp.int32)
    block_start = jnp.arange(n_blocks, dtype=jnp.int32) * bm
    block_expert = jnp.minimum(jnp.sum(block_start[:, None] >= pad_ends[None, :], axis=1), N_EXPERTS - 1)
    n_used = (pad_ends[-1:] // bm).astype(jnp.int32)
    xbuf = _scatter_rows(dest, x, n_rows, xbuf_init)
    ybuf = _expert_ffn(block_expert.astype(jnp.int32), n_used, xbuf, w1e, w3e, w2e, j)
    x2, x2b = _combine(dest, ybuf, gates.T, x, ln_g, ln_b, layer)
    return x2, x2b, xbuf


def _rotary_tables(seq):
    inv_freq = ROPE_THETA ** (-jnp.arange(0, HEAD_DIM, 2, dtype=F32) / HEAD_DIM)
    ang = jnp.arange(seq, dtype=F32)[:, None] * inv_freq[None, :]
    cos, sin = jnp.cos(ang), jnp.sin(ang)
    cos_t, sin_t = jnp.tile(cos, (1, 4)), jnp.concatenate([-sin, -sin, sin, sin], axis=1)
    q_scale = HEAD_DIM ** -0.5 * math.log2(math.e)
    return jnp.concatenate([cos_t * q_scale, sin_t * q_scale, cos_t, sin_t], axis=1)


_B_HEAD_ORDER = tuple(h for j in range(N_HEADS_B // N_KV_B) for h in (j, N_HEADS_B // N_KV_B + j))


def kernel(x, w_in, b_gate, sink_logits, w_br_a, w_br_b, w_out, ln_mix_g, ln_mix_b, w1_dense, w3_dense,
           w2_dense, w_router, w1_exp, w3_exp, w2_exp, ln_ffn_g, ln_ffn_b):
    b, s, d = x.shape
    assert b == 1 and d == D_MODEL and s % (16 * 2 * ATT_QS) == 0 and s % IN_TM == 0
    rope_t = _rotary_tables(s)
    order = jnp.asarray(_B_HEAD_ORDER)

    def rotary_layout(w, head_order=None):
        heads = w.shape[-1] // HEAD_DIM
        w = w.reshape(DEPTH, D_MODEL, heads, 2, HEAD_DIM // 2)
        if head_order is not None:
            w = w[:, :, head_order]
        w = w.reshape(DEPTH, D_MODEL, heads // 2, 2, 2, HEAD_DIM // 2).transpose(0, 1, 2, 4, 3, 5)
        return w.reshape(DEPTH, D_MODEL, heads * HEAD_DIM)

    kb0 = QKV_A + WIDTH_B
    w_in_b = jnp.concatenate([
        rotary_layout(w_in[:, :, :WIDTH_A]), rotary_layout(w_in[:, :, WIDTH_A:2 * WIDTH_A]),
        w_in[:, :, 2 * WIDTH_A:QKV_A], rotary_layout(w_in[:, :, QKV_A:kb0], order),
        rotary_layout(w_in[:, :, kb0:kb0 + WIDTH_KV_B]), w_in[:, :, kb0 + WIDTH_KV_B:]], axis=2).astype(BF16)
    w_br_b_b = w_br_b.reshape(DEPTH, N_HEADS_B, HEAD_DIM, D_MODEL)[:, order].reshape(w_br_b.shape).astype(BF16)
    w_br_a_b, w_out_b = w_br_a.astype(BF16), w_out.astype(BF16)
    n_dense, nc = w1_dense.shape[0], D_FF_DENSE // DENSE_FC
    chunk_cols = lambda w: w.astype(BF16).reshape(n_dense, D_MODEL, nc, DENSE_FC).transpose(0, 2, 1, 3)
    w1d, w3d = chunk_cols(w1_dense), chunk_cols(w3_dense)
    w2d = w2_dense.astype(BF16).reshape(n_dense, nc, DENSE_FC, D_MODEL)
    w1e, w3e, w2e = w1_exp.astype(BF16), w3_exp.astype(BF16), w2_exp.astype(BF16)
    wr_pad = jnp.pad(w_router, ((0, 0), (0, 0), (0, LANES - N_EXPERTS)))
    wr_hi = wr_pad.astype(BF16)
    wr_lo = (wr_pad - wr_hi.astype(F32)).astype(BF16)
    vec3 = lambda v: v.reshape(v.shape[0], 1, v.shape[1])
    b_gate3, lmg, lmb, lfg, lfb = map(vec3, (b_gate, ln_mix_g, ln_mix_b, ln_ffn_g, ln_ffn_b))

    xf = x.reshape(s, d)
    xb = xf.astype(BF16)
    xbuf = None
    for i in range(DEPTH):
        qkv1, qkv4, qkv16, qb, kvb, g = _inproj(xb, w_in_b, i, rope_t)
        o1, l1 = _attn_a(qkv1.reshape(1, s, QKV_A))
        o4, l4 = _attn_a(qkv4)
        o16, l16 = _attn_a(qkv16)
        yb = _attn_b(qb, kvb, sink_logits[i])
        xf, xb = _merge(o1.reshape(s, WIDTH_A), l1.reshape(s, WIDTH_A), o4, l4, o16, l16, yb, g, xf,
                        b_gate3, w_br_a_b, w_br_b_b, w_out_b, lmg, lmb, i)
        j = i // 2
        if i % 2 == 0:
            xf, xb = _dense_ffn(xb, xf, w1d, w3d, w2d, lfg, lfb, i, j)
        else:
            xf, xb, xbuf = _moe_ffn(xf, wr_hi[j], wr_lo[j], w1e, w3e, w2e, lfg, lfb, i, j, xbuf_init=xbuf)
    return xf.reshape(b, s, d)
```

```python
import functools
import math

import jax
import jax.numpy as jnp
from jax import lax
from jax.experimental import pallas as pl
from jax.experimental.pallas import tpu as pltpu

F32 = jnp.float32
BF16 = jnp.bfloat16

D_MODEL = 1024
DEPTH = 4
HEAD_DIM = 64
N_HEADS_A = 8
DILATIONS = (1, 4, 16)
HALF_W_A = 64
N_HEADS_B = 8
N_KV_B = 2
HALF_W_B = 128
ROPE_THETA = 10000.0
WIDTH_A = N_HEADS_A * HEAD_DIM
WIDTH_B = N_HEADS_B * HEAD_DIM
WIDTH_KV_B = N_KV_B * HEAD_DIM
QKV_A = 3 * WIDTH_A
IN_COLS = QKV_A + WIDTH_B + 2 * WIDTH_KV_B + 2 * D_MODEL
D_FF_DENSE = 2816
N_EXPERTS = 8
D_FF_EXPERT = 3584
ALPHA = (2 * DEPTH) ** 0.25
LN_EPS = 1e-5
NEG_INF = -1e30

LANES = 128
VMEM_LIMIT = 56 * 1024 * 1024

IN_TM, IN_TN = 512, 256
ATT_QS = 128
ATT_A_UNROLL, ATT_B_UNROLL = 8, 2
MERGE_TM = 512
DENSE_TM, DENSE_FC = 1024, 256
ROUTE_TM = 512
MOE_BM = 512
MOE_FC = 1792
MOE_SUB = 256
SCATTER_TM = 512
COMBINE_TM = 512
DMA_UNROLL = 8


def _params(sem):
    return pltpu.CompilerParams(dimension_semantics=sem, vmem_limit_bytes=VMEM_LIMIT)


def _layer_norm(z, g, b):
    mu = jnp.mean(z, axis=-1, keepdims=True)
    zc = z - mu
    var = jnp.mean(zc * zc, axis=-1, keepdims=True)
    return zc * lax.rsqrt(var + LN_EPS) * g + b


def _rope(a, cos, sin):
    return a * cos + pltpu.roll(a, LANES // 2, axis=1) * sin


def _inproj_kernel(x_ref, w_ref, rope_ref, qkv1_ref, qkv4_ref, qkv16_ref, qb_ref, kvb_ref, g_ref, rs_ref):
    tm = x_ref.shape[0]
    x = x_ref[...]
    rope_q = lambda v: _rope(v, rope_ref[:, 0:LANES], rope_ref[:, LANES:2 * LANES])
    rope_k = lambda v: _rope(v, rope_ref[:, 2 * LANES:3 * LANES], rope_ref[:, 3 * LANES:])
    plain = lambda v: v
    n_q, n_a = WIDTH_A // LANES, QKV_A // LANES
    qb0 = n_a
    kb0 = qb0 + WIDTH_B // LANES
    vb0 = kb0 + WIDTH_KV_B // LANES
    g0 = vb0 + WIDTH_KV_B // LANES
    per_dot = IN_TN // LANES

    for t in range(IN_COLS // IN_TN):
        acc = jnp.dot(x, w_ref[:, t * IN_TN:(t + 1) * IN_TN], preferred_element_type=F32)
        for h in range(per_dot):
            c = t * per_dot + h
            blk = acc[:, h * LANES:(h + 1) * LANES]
            if c < n_a:
                post = rope_q if c < n_q else rope_k if c < 2 * n_q else plain
                sl = slice(c * LANES, (c + 1) * LANES)
                rs_ref[c] = post(blk)
                qkv1_ref[:, sl] = rs_ref[c].astype(BF16)
                for r, ref in ((4, qkv4_ref), (16, qkv16_ref)):
                    for res in range(r):
                        ref[res, :, sl] = rs_ref[c, pl.ds(res, tm // r, stride=r), :].astype(BF16)
            elif c < kb0:
                qb_ref[:, (c - qb0) * LANES:(c - qb0 + 1) * LANES] = rope_q(blk).astype(BF16)
            elif c < vb0:
                kvb_ref[:, :LANES] = rope_k(blk).astype(BF16)
            elif c < g0:
                kvb_ref[:, LANES:] = blk.astype(BF16)
            else:
                g_ref[:, (c - g0) * LANES:(c - g0 + 1) * LANES] = blk.astype(BF16)


def _inproj(xb, w_in_b, layer, rope_t):
    s = xb.shape[0]
    tm = IN_TM
    out_shape = (
        jax.ShapeDtypeStruct((s, QKV_A), BF16),
        jax.ShapeDtypeStruct((4, s // 4, QKV_A), BF16),
        jax.ShapeDtypeStruct((16, s // 16, QKV_A), BF16),
        jax.ShapeDtypeStruct((s, WIDTH_B), BF16),
        jax.ShapeDtypeStruct((s, 2 * WIDTH_KV_B), BF16),
        jax.ShapeDtypeStruct((s, 2 * D_MODEL), BF16),
    )
    row = lambda w: pl.BlockSpec((tm, w), lambda i: (i, 0))
    out_specs = (
        row(QKV_A),
        pl.BlockSpec((4, tm // 4, QKV_A), lambda i: (0, i, 0)),
        pl.BlockSpec((16, tm // 16, QKV_A), lambda i: (0, i, 0)),
        row(WIDTH_B), row(2 * WIDTH_KV_B), row(2 * D_MODEL),
    )
    in_specs = [
        row(D_MODEL),
        pl.BlockSpec((None, D_MODEL, IN_COLS), lambda i: (layer, 0, 0), pipeline_mode=pl.Buffered(1)),
        row(4 * LANES),
    ]
    return pl.pallas_call(
        _inproj_kernel, out_shape=out_shape, grid=(s // tm,), in_specs=in_specs, out_specs=out_specs,
        scratch_shapes=[pltpu.VMEM((QKV_A // LANES, tm, LANES), F32)],
        compiler_params=_params(("arbitrary",)), name="inproj",
    )(xb, w_in_b, rope_t)


def _pair_attention(q_pairs, k_pairs, v_pairs, valid, sinks=None):
    qs = q_pairs[0].shape[0]
    lane = lax.broadcasted_iota(jnp.int32, (qs, LANES), 1)
    lo = lane < HEAD_DIM
    q_first = (lane % HEAD_DIM) < HEAD_DIM // 2
    nt = (((1,), (1,)), ((), ()))
    scores = []
    for qp, kp in zip(q_pairs, k_pairs):
        for half in (q_first, ~q_first):
            scores.append(lax.dot_general(jnp.where(half, qp, jnp.zeros_like(qp)), kp, nt,
                                          preferred_element_type=F32))
    stats = []
    for idx, s in enumerate(scores):
        s = jnp.where(valid, s, NEG_INF)
        m = jnp.max(s, axis=-1, keepdims=True)
        if sinks is not None:
            m = jnp.maximum(m, sinks[idx])
        p = jnp.exp2(s - m)
        d = jnp.sum(p, axis=-1, keepdims=True)
        if sinks is not None:
            d = d + jnp.exp2(sinks[idx] - m)
        stats.append((p.astype(BF16), m, d))
    outs, lses = [], []
    for j, vp in enumerate(v_pairs):
        (p0, m0, d0), (p1, m1, d1) = stats[2 * j], stats[2 * j + 1]
        o0 = jnp.dot(p0, vp, preferred_element_type=F32)
        o1 = jnp.dot(p1, vp, preferred_element_type=F32)
        d = jnp.where(lo, d0, d1)
        outs.append(jnp.where(lo, o0, o1) / d)
        lses.append(jnp.where(lo, m0, m1) + jnp.log2(d))
    return outs, lses


def _fill_slab(slab_ref, prev_ref, main_ref, next_ref):
    w, bq = prev_ref.shape[0], main_ref.shape[0]
    slab_ref[0:w] = prev_ref[...]
    slab_ref[w:w + bq] = main_ref[...]
    slab_ref[w + bq:] = next_ref[...]


def _band_mask(block_row0, off, n, half_w, qs):
    kw = qs + 2 * half_w
    row = lax.broadcasted_iota(jnp.int32, (qs, kw), 0)
    col = lax.broadcasted_iota(jnp.int32, (qs, kw), 1)
    kpos = block_row0 + off - half_w + col
    return (jnp.abs(col - half_w - row) <= half_w) & (kpos >= 0) & (kpos < n)


def _attn_a_kernel(q_ref, km_ref, kp_ref, kn_ref, vm_ref, vp_ref, vn_ref, o_ref, l_ref, ks_ref, vs_ref, *, n):
    i = pl.program_id(1)
    bq = q_ref.shape[0]
    qs, w = ATT_QS, HALF_W_A
    kw = qs + 2 * w
    n_pair = WIDTH_A // LANES
    _fill_slab(ks_ref, kp_ref, km_ref, kn_ref)
    _fill_slab(vs_ref, vp_ref, vm_ref, vn_ref)

    def body(sb, carry):
        off = pl.multiple_of(sb * qs, qs)
        valid = _band_mask(i * bq, off, n, w, qs)
        q = q_ref[pl.ds(off, qs), :]
        kwin = ks_ref[pl.ds(off, kw), :]
        vwin = vs_ref[pl.ds(off, kw), :]
        pairs = [slice(j * LANES, (j + 1) * LANES) for j in range(n_pair)]
        outs, lses = _pair_attention([q[:, sl] for sl in pairs], [kwin[:, sl] for sl in pairs],
                                     [vwin[:, sl] for sl in pairs], valid)
        for sl, o, lse in zip(pairs, outs, lses):
            o_ref[pl.ds(off, qs), sl] = o.astype(BF16)
            l_ref[pl.ds(off, qs), sl] = lse
        return carry

    lax.fori_loop(0, bq // qs, body, 0, unroll=ATT_A_UNROLL)


def _halo_specs(bq, w, n, width, col):
    nb = bq // w
    main = pl.BlockSpec((None, bq, width), lambda c, i: (c, i, col))
    prev = pl.BlockSpec((None, w, width), lambda c, i: (c, jnp.maximum(i * nb - 1, 0), col))
    nxt = pl.BlockSpec((None, w, width), lambda c, i: (c, jnp.minimum((i + 1) * nb, n // w - 1), col))
    return [main, prev, nxt]


def _attn_a(qkv_r):
    r, n, _ = qkv_r.shape
    bq = min(n, 1024)
    w = HALF_W_A
    blk = lambda: pl.BlockSpec((None, bq, WIDTH_A), lambda c, i: (c, i, 0))
    in_specs = [blk()] + _halo_specs(bq, w, n, WIDTH_A, 1) + _halo_specs(bq, w, n, WIDTH_A, 2)
    out_shape = (jax.ShapeDtypeStruct((r, n, WIDTH_A), BF16), jax.ShapeDtypeStruct((r, n, WIDTH_A), F32))
    return pl.pallas_call(
        functools.partial(_attn_a_kernel, n=n), out_shape=out_shape, grid=(r, n // bq),
        in_specs=in_specs, out_specs=(blk(), blk()),
        scratch_shapes=[pltpu.VMEM((bq + 2 * w, WIDTH_A), BF16)] * 2,
        compiler_params=_params(("arbitrary",) * 2), name=f"attn_a_r{r}",
    )(*([qkv_r] * 7))


def _attn_b_kernel(sink_ref, q_ref, kvm_ref, kvp_ref, kvn_ref, o_ref, slab_ref, *, n):
    i = pl.program_id(1)
    bq = q_ref.shape[0]
    qs, w = ATT_QS, HALF_W_B
    kw = qs + 2 * w
    n_pair = WIDTH_B // LANES
    _fill_slab(slab_ref, kvp_ref, kvm_ref, kvn_ref)
    log2e = math.log2(math.e)
    sinks = [sink_ref[hh * n_pair + j] * log2e for j in range(n_pair) for hh in range(2)]

    def body(sb, carry):
        off = pl.multiple_of(sb * qs, qs)
        valid = _band_mask(i * bq, off, n, w, qs)
        q = q_ref[pl.ds(off, qs), :]
        win = slab_ref[pl.ds(off, kw), :]
        kb, vb = win[:, :LANES], win[:, LANES:]
        pairs = [slice(j * LANES, (j + 1) * LANES) for j in range(n_pair)]
        outs, _ = _pair_attention([q[:, sl] for sl in pairs], [kb] * n_pair, [vb] * n_pair, valid, sinks)
        for sl, o in zip(pairs, outs):
            o_ref[pl.ds(off, qs), sl] = o.astype(BF16)
        return carry

    lax.fori_loop(0, bq // qs, body, 0, unroll=ATT_B_UNROLL)


def _attn_b(qb, kvb, sink):
    s = qb.shape[0]
    bq = min(s, 1024)
    w = HALF_W_B
    kv_w = 2 * WIDTH_KV_B
    drop = lambda spec: pl.BlockSpec(spec.block_shape, lambda c, i, sk, f=spec.index_map: f(c, i))
    grid_spec = pltpu.PrefetchScalarGridSpec(
        num_scalar_prefetch=1, grid=(1, s // bq),
        in_specs=[pl.BlockSpec((None, bq, WIDTH_B), lambda c, i, sk: (c, i, 0))]
        + [drop(sp) for sp in _halo_specs(bq, w, s, kv_w, 0)],
        out_specs=pl.BlockSpec((None, bq, WIDTH_B), lambda c, i, sk: (c, i, 0)),
        scratch_shapes=[pltpu.VMEM((bq + 2 * w, kv_w), BF16)],
    )
    kvb3 = kvb.reshape(1, s, kv_w)
    out = pl.pallas_call(
        functools.partial(_attn_b_kernel, n=s), out_shape=jax.ShapeDtypeStruct((1, s, WIDTH_B), BF16),
        grid_spec=grid_spec, compiler_params=_params(("arbitrary",) * 2), name="attn_b",
    )(sink, qb.reshape(1, s, WIDTH_B), kvb3, kvb3, kvb3)
    return out.reshape(s, WIDTH_B)


def _merge_kernel(o1_ref, l1_ref, o4_ref, l4_ref, o16_ref, l16_ref, yb_ref, g_ref, x_ref, bg_ref,
                  wa_ref, wb_ref, wo_ref, lg_ref, lb_ref, x1_ref, x1b_ref, so4, sl4, so16, sl16):
    tm = x_ref.shape[0]
    for r, src_o, src_l, dst_o, dst_l in ((4, o4_ref, l4_ref, so4, sl4), (16, o16_ref, l16_ref, so16, sl16)):
        for c in range(r):
            for h in range(WIDTH_A // LANES):
                sl = slice(h * LANES, (h + 1) * LANES)
                dst_o[h, pl.ds(c, tm // r, stride=r), :] = src_o[c, :, sl].astype(F32)
                dst_l[h, pl.ds(c, tm // r, stride=r), :] = src_l[c, :, sl]
    slabs = lambda ref: jnp.concatenate([ref[h] for h in range(WIDTH_A // LANES)], axis=1)
    l1, l4, l16 = l1_ref[...], slabs(sl4), slabs(sl16)
    lmax = jnp.maximum(jnp.maximum(l1, l4), l16)
    e1, e4, e16 = jnp.exp2(l1 - lmax), jnp.exp2(l4 - lmax), jnp.exp2(l16 - lmax)
    y_a = (e1 * o1_ref[...].astype(F32) + e4 * slabs(so4) + e16 * slabs(so16)) / (e1 + e4 + e16)
    ta = jnp.dot(y_a.astype(BF16), wa_ref[...], preferred_element_type=F32)
    tb = jnp.dot(yb_ref[...], wb_ref[...], preferred_element_type=F32)
    gates = jax.nn.sigmoid(g_ref[...].astype(F32) + bg_ref[...])
    merged = gates[:, :D_MODEL] * ta + gates[:, D_MODEL:] * tb
    mix = jnp.dot(merged.astype(BF16), wo_ref[...], preferred_element_type=F32)
    x1 = _layer_norm(ALPHA * x_ref[...] + mix, lg_ref[...], lb_ref[...])
    x1_ref[...] = x1
    x1b_ref[...] = x1.astype(BF16)


def _merge(o1, l1, o4, l4, o16, l16, yb, g, x, b_gate, w_br_a_b, w_br_b_b, w_out_b, ln_g, ln_b, layer):
    s = x.shape[0]
    tm = MERGE_TM
    row = lambda w: pl.BlockSpec((tm, w), lambda i: (i, 0))
    perm = lambda r: pl.BlockSpec((r, tm // r, WIDTH_A), lambda i: (0, i, 0))
    vec = lambda w: pl.BlockSpec((None, 1, w), lambda i: (layer, 0, 0))
    wsp = lambda k, n: pl.BlockSpec((None, k, n), lambda i: (layer, 0, 0))
    in_specs = [row(WIDTH_A), row(WIDTH_A), perm(4), perm(4), perm(16), perm(16), row(WIDTH_B),
                row(2 * D_MODEL), row(D_MODEL), vec(2 * D_MODEL),
                wsp(WIDTH_A, D_MODEL), wsp(WIDTH_B, D_MODEL), wsp(D_MODEL, D_MODEL),
                vec(D_MODEL), vec(D_MODEL)]
    out_shape = (jax.ShapeDtypeStruct((s, D_MODEL), F32), jax.ShapeDtypeStruct((s, D_MODEL), BF16))
    return pl.pallas_call(
        _merge_kernel, out_shape=out_shape, grid=(s // tm,), in_specs=in_specs,
        out_specs=(row(D_MODEL), row(D_MODEL)),
        scratch_shapes=[pltpu.VMEM((WIDTH_A // LANES, tm, LANES), F32)] * 4,
        compiler_params=_params(("arbitrary",)), name="merge_outproj_ln",
    )(o1, l1, o4, l4, o16, l16, yb, g, x, b_gate, w_br_a_b, w_br_b_b, w_out_b, ln_g, ln_b)


def _swiglu(xb, w1, w3):
    h = jnp.dot(xb, w1, preferred_element_type=F32)
    u = jnp.dot(xb, w3, preferred_element_type=F32)
    return (h * jax.nn.sigmoid(h) * u).astype(BF16)


def _dense_kernel(xb_ref, x_ref, w1_ref, w3_ref, w2_ref, lg_ref, lb_ref, x2_ref, x2b_ref, acc_ref):
    n_chunks = w1_ref.shape[0]
    acc_ref[...] = jnp.zeros_like(acc_ref)

    def body(c, carry):
        a = _swiglu(xb_ref[...], w1_ref[c], w3_ref[c])
        acc_ref[...] += jnp.dot(a, w2_ref[c], preferred_element_type=F32)
        return carry

    lax.fori_loop(0, n_chunks, body, 0)
    x2 = _layer_norm(ALPHA * x_ref[...] + acc_ref[...], lg_ref[...], lb_ref[...])
    x2_ref[...] = x2
    x2b_ref[...] = x2.astype(BF16)


def _dense_ffn(xb, x, w1d, w3d, w2d, ln_g, ln_b, layer, j):
    s = x.shape[0]
    tm, fc = DENSE_TM, DENSE_FC
    nc = D_FF_DENSE // fc
    row = lambda: pl.BlockSpec((tm, D_MODEL), lambda i: (i, 0))
    vec = lambda: pl.BlockSpec((None, 1, D_MODEL), lambda i: (layer, 0, 0))
    resident = dict(pipeline_mode=pl.Buffered(1))
    in_specs = [row(), row(),
                pl.BlockSpec((None, nc, D_MODEL, fc), lambda i: (j, 0, 0, 0), **resident),
                pl.BlockSpec((None, nc, D_MODEL, fc), lambda i: (j, 0, 0, 0), **resident),
                pl.BlockSpec((None, nc, fc, D_MODEL), lambda i: (j, 0, 0, 0), **resident),
                vec(), vec()]
    out_shape = (jax.ShapeDtypeStruct((s, D_MODEL), F32), jax.ShapeDtypeStruct((s, D_MODEL), BF16))
    return pl.pallas_call(
        _dense_kernel, out_shape=out_shape, grid=(s // tm,), in_specs=in_specs,
        out_specs=(row(), row()), scratch_shapes=[pltpu.VMEM((tm, D_MODEL), F32)],
        compiler_params=_params(("arbitrary",)), name="dense_ffn_ln",
    )(xb, x, w1d, w3d, w2d, ln_g, ln_b)


def _split_hi_lo(v):
    c = v * (2.0 ** 16 + 1.0)
    hi = c - (c - v)
    return hi.astype(BF16), (v - hi).astype(BF16)


def _router_kernel(x_ref, wh_ref, wl_ref, idx_ref, gate_ref, rank_ref, cnt_ref, carry_ref):
    i = pl.program_id(0)
    tm = x_ref.shape[0]

    @pl.when(i == 0)
    def _():
        carry_ref[...] = jnp.zeros_like(carry_ref)

    xh, xl = _split_hi_lo(x_ref[...])
    logits = (jnp.dot(xh, wh_ref[...], preferred_element_type=F32)
              + jnp.dot(xl, wh_ref[...], preferred_element_type=F32)
              + jnp.dot(xh, wl_ref[...], preferred_element_type=F32))
    lt = logits.T[:N_EXPERTS, :]
    ii = lax.broadcasted_iota(jnp.int32, lt.shape, 0)
    m1 = jnp.max(lt, axis=0, keepdims=True)
    i1 = jnp.min(jnp.where(lt == m1, ii, N_EXPERTS), axis=0, keepdims=True)
    l2 = jnp.where(ii == i1, -jnp.inf, lt)
    m2 = jnp.max(l2, axis=0, keepdims=True)
    i2 = jnp.min(jnp.where(l2 == m2, ii, N_EXPERTS), axis=0, keepdims=True)
    e = jnp.exp(m2 - m1)
    gate_ref[0:1, :] = 1.0 / (1.0 + e)
    gate_ref[1:2, :] = e / (1.0 + e)
    idx_ref[0:1, :] = i1
    idx_ref[1:2, :] = i2
    member = (ii == i1) | (ii == i2)
    a = lax.broadcasted_iota(jnp.int32, (tm, tm), 0)
    b = lax.broadcasted_iota(jnp.int32, (tm, tm), 1)
    upper = jnp.where(a < b, 1.0, 0.0).astype(BF16)
    memf = jnp.where(member, 1.0, 0.0)
    rank = jnp.dot(memf.astype(BF16), upper, preferred_element_type=F32) + carry_ref[:, 0:1]
    rank_ref[0:1, :] = jnp.sum(jnp.where(ii == i1, rank, 0.0), axis=0, keepdims=True).astype(jnp.int32)
    rank_ref[1:2, :] = jnp.sum(jnp.where(ii == i2, rank, 0.0), axis=0, keepdims=True).astype(jnp.int32)
    carry_ref[...] = carry_ref[...] + jnp.sum(memf, axis=1, keepdims=True)
    cnt_ref[...] = carry_ref[...].astype(jnp.int32)


def _router(x, wr_hi, wr_lo):
    t = x.shape[0]
    tm = ROUTE_TM
    two = lambda: pl.BlockSpec((2, tm), lambda i: (0, i))
    wsp = lambda: pl.BlockSpec((D_MODEL, LANES), lambda i: (0, 0))
    out_shape = (jax.ShapeDtypeStruct((2, t), jnp.int32), jax.ShapeDtypeStruct((2, t), F32),
                 jax.ShapeDtypeStruct((2, t), jnp.int32), jax.ShapeDtypeStruct((N_EXPERTS, LANES), jnp.int32))
    return pl.pallas_call(
        _router_kernel, out_shape=out_shape, grid=(t // tm,),
        in_specs=[pl.BlockSpec((tm, D_MODEL), lambda i: (i, 0)), wsp(), wsp()],
        out_specs=(two(), two(), two(), pl.BlockSpec((N_EXPERTS, LANES), lambda i: (0, 0))),
        scratch_shapes=[pltpu.VMEM((N_EXPERTS, LANES), F32)],
        compiler_params=_params(("arbitrary",)), name="router",
    )(x, wr_hi, wr_lo)


def _row_copy(src_ref, src_row, dst_ref, dst_row, sem):
    return pltpu.make_async_copy(src_ref.at[pl.ds(src_row, 1)], dst_ref.at[pl.ds(dst_row, 1)], sem)


def _scatter_kernel(dest_ref, x_ref, zeros_ref, xbuf_ref, stage_ref, sems):
    del zeros_ref
    tm = x_ref.shape[0]
    i, n = pl.program_id(0), pl.num_programs(0)
    t = n * tm
    slot = i % 2
    stage_ref[slot] = x_ref[...]

    def start(r, carry):
        for k in range(2):
            _row_copy(stage_ref.at[slot], r, xbuf_ref, dest_ref[k * t + i * tm + r], sems.at[slot]).start(priority=k)
        return carry

    def drain(s):
        def wait(r, carry):
            for k in range(2):
                _row_copy(stage_ref.at[s], 0, xbuf_ref, 0, sems.at[s]).wait()
            return carry
        lax.fori_loop(0, tm, wait, 0, unroll=DMA_UNROLL)

    lax.fori_loop(0, tm, start, 0, unroll=DMA_UNROLL)

    @pl.when(i > 0)
    def _():
        drain(1 - slot)

    @pl.when(i == n - 1)
    def _():
        drain(slot)


def _scatter_rows(dest, x, n_rows, init=None):
    t = x.shape[0]
    tm = SCATTER_TM
    zeros = jnp.zeros((n_rows, D_MODEL), x.dtype) if init is None else init
    grid_spec = pltpu.PrefetchScalarGridSpec(
        num_scalar_prefetch=1, grid=(t // tm,),
        in_specs=[pl.BlockSpec((tm, D_MODEL), lambda i, d: (i, 0)), pl.BlockSpec(memory_space=pl.ANY)],
        out_specs=pl.BlockSpec(memory_space=pl.ANY),
        scratch_shapes=[pltpu.VMEM((2, tm, D_MODEL), x.dtype), pltpu.SemaphoreType.DMA((2,))],
    )
    return pl.pallas_call(
        _scatter_kernel, out_shape=jax.ShapeDtypeStruct((n_rows, D_MODEL), x.dtype), grid_spec=grid_spec,
        input_output_aliases={2: 0},
        compiler_params=pltpu.CompilerParams(dimension_semantics=("arbitrary",), has_side_effects=True,
                                             disable_bounds_checks=True),
        name="moe_scatter",
    )(dest, x, zeros)


def _expert_kernel(be_ref, nu_ref, x_ref, w1_ref, w3_ref, w2_ref, y_ref, xb_ref, acc_ref):
    i, c = pl.program_id(0), pl.program_id(1)

    @pl.when(i < nu_ref[0])
    def _():
        @pl.when(c == 0)
        def _():
            xb_ref[...] = x_ref[...].astype(BF16)
            acc_ref[...] = jnp.zeros_like(acc_ref)

        for k in range(MOE_FC // MOE_SUB):
            sl = slice(k * MOE_SUB, (k + 1) * MOE_SUB)
            a = _swiglu(xb_ref[...], w1_ref[:, sl], w3_ref[:, sl])
            acc_ref[...] += jnp.dot(a, w2_ref[sl, :], preferred_element_type=F32)

        @pl.when(c == pl.num_programs(1) - 1)
        def _():
            y_ref[...] = acc_ref[...]

    @pl.when(i >= nu_ref[0])
    def _():
        y_ref[...] = jnp.zeros_like(y_ref)


def _expert_ffn(block_expert, n_used, xbuf, w1e, w3e, w2e, j):
    n_rows = xbuf.shape[0]
    bm, fc = MOE_BM, MOE_FC
    nc = D_FF_EXPERT // fc
    row = lambda i, be, nu: jnp.minimum(i, jnp.maximum(nu[0] - 1, 0))
    col = lambda i, c, nu: jnp.where(i < nu[0], c, nc - 1)
    grid_spec = pltpu.PrefetchScalarGridSpec(
        num_scalar_prefetch=2, grid=(n_rows // bm, nc),
        in_specs=[
            pl.BlockSpec((bm, D_MODEL), lambda i, c, be, nu: (row(i, be, nu), 0)),
            pl.BlockSpec((None, None, D_MODEL, fc), lambda i, c, be, nu: (j, be[row(i, be, nu)], 0, col(i, c, nu))),
            pl.BlockSpec((None, None, D_MODEL, fc), lambda i, c, be, nu: (j, be[row(i, be, nu)], 0, col(i, c, nu))),
            pl.BlockSpec((None, None, fc, D_MODEL), lambda i, c, be, nu: (j, be[row(i, be, nu)], col(i, c, nu), 0)),
        ],
        out_specs=pl.BlockSpec((bm, D_MODEL), lambda i, c, be, nu: (i, 0)),
        scratch_shapes=[pltpu.VMEM((bm, D_MODEL), BF16), pltpu.VMEM((bm, D_MODEL), F32)],
    )
    return pl.pallas_call(
        _expert_kernel, out_shape=jax.ShapeDtypeStruct((n_rows, D_MODEL), F32), grid_spec=grid_spec,
        compiler_params=_params(("arbitrary", "arbitrary")), name="expert_ffn",
    )(block_expert, n_used, xbuf, w1e, w3e, w2e)


def _combine_kernel(dest_ref, ybuf_ref, gate_ref, x_ref, lg_ref, lb_ref, x2_ref, x2b_ref, buf_ref, sems):
    tm = x_ref.shape[0]
    i, n = pl.program_id(0), pl.num_programs(0)
    t = n * tm
    slot = i % 2

    def issue(step, s):
        def start(r, carry):
            for k in range(2):
                _row_copy(ybuf_ref, dest_ref[k * t + step * tm + r], buf_ref.at[s, k], r, sems.at[s]).start(priority=k)
            return carry
        lax.fori_loop(0, tm, start, 0, unroll=DMA_UNROLL)

    @pl.when(i == 0)
    def _():
        issue(0, 0)

    @pl.when(i + 1 < n)
    def _():
        issue(i + 1, 1 - slot)

    def wait(r, carry):
        for k in range(2):
            _row_copy(ybuf_ref, 0, buf_ref.at[slot, k], 0, sems.at[slot]).wait()
        return carry

    lax.fori_loop(0, tm, wait, 0, unroll=DMA_UNROLL)
    g = gate_ref[...]
    y = buf_ref[slot, 0] * g[:, 0:1] + buf_ref[slot, 1] * g[:, 1:2]
    x2 = _layer_norm(ALPHA * x_ref[...] + y, lg_ref[...], lb_ref[...])
    x2_ref[...] = x2
    x2b_ref[...] = x2.astype(BF16)


def _combine(dest, ybuf, gates_t, x, ln_g, ln_b, layer):
    t = x.shape[0]
    tm = COMBINE_TM
    row = lambda: pl.BlockSpec((tm, D_MODEL), lambda i, d: (i, 0))
    vec = lambda: pl.BlockSpec((None, 1, D_MODEL), lambda i, d: (layer, 0, 0))
    grid_spec = pltpu.PrefetchScalarGridSpec(
        num_scalar_prefetch=1, grid=(t // tm,),
        in_specs=[pl.BlockSpec(memory_space=pl.ANY), pl.BlockSpec((tm, 2), lambda i, d: (i, 0)), row(), vec(), vec()],
        out_specs=(row(), row()),
        scratch_shapes=[pltpu.VMEM((2, 2, tm, D_MODEL), F32), pltpu.SemaphoreType.DMA((2,))],
    )
    out_shape = (jax.ShapeDtypeStruct((t, D_MODEL), F32), jax.ShapeDtypeStruct((t, D_MODEL), BF16))
    return pl.pallas_call(
        _combine_kernel, out_shape=out_shape, grid_spec=grid_spec,
        compiler_params=pltpu.CompilerParams(dimension_semantics=("arbitrary",), vmem_limit_bytes=VMEM_LIMIT,
                                             disable_bounds_checks=True),
        name="moe_combine_ln",
    )(dest, ybuf, gates_t, x, ln_g, ln_b)


def _moe_ffn(x, wr_hi, wr_lo, w1e, w3e, w2e, ln_g, ln_b, layer, j, xbuf_init=None):
    t = x.shape[0]
    bm = MOE_BM
    idx, gates, rank, cnt = _router(x, wr_hi, wr_lo)
    counts = cnt[:, 0]
    padded = (counts + bm - 1) // bm * bm
    pad_ends = jnp.cumsum(padded)
    pad_starts = pad_ends - padded
    n_rows = -(-(2 * t + N_EXPERTS * (bm - 1)) // bm) * bm
    n_blocks = n_rows // bm
    expert_ids = jnp.arange(N_EXPERTS, dtype=jnp.int32)[:, None, None]
    start_of = jnp.sum(jnp.where(idx[None] == expert_ids, pad_starts[:, None, None], 0), axis=0)
    dest = (start_of + rank).reshape(-1).astype(jnp.int32)
    block_start = jnp.arange(n_blocks, dtype=jnp.int32) * bm
    block_expert = jnp.minimum(jnp.sum(block_start[:, None] >= pad_ends[None, :], axis=1), N_EXPERTS - 1)
    n_used = (pad_ends[-1:] // bm).astype(jnp.int32)
    xbuf = _scatter_rows(dest, x, n_rows, xbuf_init)
    ybuf = _expert_ffn(block_expert.astype(jnp.int32), n_used, xbuf, w1e, w3e, w2e, j)
    x2, x2b = _combine(dest, ybuf, gates.T, x, ln_g, ln_b, layer)
    return x2, x2b, xbuf


def _rotary_tables(seq):
    inv_freq = ROPE_THETA ** (-jnp.arange(0, HEAD_DIM, 2, dtype=F32) / HEAD_DIM)
    ang = jnp.arange(seq, dtype=F32)[:, None] * inv_freq[None, :]
    cos, sin = jnp.cos(ang), jnp.sin(ang)
    cos_t, sin_t = jnp.tile(cos, (1, 4)), jnp.concatenate([-sin, -sin, sin, sin], axis=1)
    q_scale = HEAD_DIM ** -0.5 * math.log2(math.e)
    return jnp.concatenate([cos_t * q_scale, sin_t * q_scale, cos_t, sin_t], axis=1)


_B_HEAD_ORDER = tuple(h for j in range(N_HEADS_B // N_KV_B) for h in (j, N_HEADS_B // N_KV_B + j))


def kernel(x, w_in, b_gate, sink_logits, w_br_a, w_br_b, w_out, ln_mix_g, ln_mix_b, w1_dense, w3_dense,
           w2_dense, w_router, w1_exp, w3_exp, w2_exp, ln_ffn_g, ln_ffn_b):
    b, s, d = x.shape
    assert b == 1 and d == D_MODEL and s % (16 * 2 * ATT_QS) == 0 and s % IN_TM == 0
    rope_t = _rotary_tables(s)
    order = jnp.asarray(_B_HEAD_ORDER)

    def rotary_layout(w, head_order=None):
        heads = w.shape[-1] // HEAD_DIM
        w = w.reshape(DEPTH, D_MODEL, heads, 2, HEAD_DIM // 2)
        if head_order is not None:
            w = w[:, :, head_order]
        w = w.reshape(DEPTH, D_MODEL, heads // 2, 2, 2, HEAD_DIM // 2).transpose(0, 1, 2, 4, 3, 5)
        return w.reshape(DEPTH, D_MODEL, heads * HEAD_DIM)

    kb0 = QKV_A + WIDTH_B
    w_in_b = jnp.concatenate([
        rotary_layout(w_in[:, :, :WIDTH_A]), rotary_layout(w_in[:, :, WIDTH_A:2 * WIDTH_A]),
        w_in[:, :, 2 * WIDTH_A:QKV_A], rotary_layout(w_in[:, :, QKV_A:kb0], order),
        rotary_layout(w_in[:, :, kb0:kb0 + WIDTH_KV_B]), w_in[:, :, kb0 + WIDTH_KV_B:]], axis=2).astype(BF16)
    w_br_b_b = w_br_b.reshape(DEPTH, N_HEADS_B, HEAD_DIM, D_MODEL)[:, order].reshape(w_br_b.shape).astype(BF16)
    w_br_a_b, w_out_b = w_br_a.astype(BF16), w_out.astype(BF16)
    n_dense, nc = w1_dense.shape[0], D_FF_DENSE // DENSE_FC
    chunk_cols = lambda w: w.astype(BF16).reshape(n_dense, D_MODEL, nc, DENSE_FC).transpose(0, 2, 1, 3)
    w1d, w3d = chunk_cols(w1_dense), chunk_cols(w3_dense)
    w2d = w2_dense.astype(BF16).reshape(n_dense, nc, DENSE_FC, D_MODEL)
    w1e, w3e, w2e = w1_exp.astype(BF16), w3_exp.astype(BF16), w2_exp.astype(BF16)
    wr_pad = jnp.pad(w_router, ((0, 0), (0, 0), (0, LANES - N_EXPERTS)))
    wr_hi, wr_lo = _split_hi_lo(wr_pad)
    vec3 = lambda v: v.reshape(v.shape[0], 1, v.shape[1])
    b_gate3, lmg, lmb, lfg, lfb = map(vec3, (b_gate, ln_mix_g, ln_mix_b, ln_ffn_g, ln_ffn_b))

    xf = x.reshape(s, d)
    xb = xf.astype(BF16)
    xbuf = None
    for i in range(DEPTH):
        qkv1, qkv4, qkv16, qb, kvb, g = _inproj(xb, w_in_b, i, rope_t)
        o1, l1 = _attn_a(qkv1.reshape(1, s, QKV_A))
        o4, l4 = _attn_a(qkv4)
        o16, l16 = _attn_a(qkv16)
        yb = _attn_b(qb, kvb, sink_logits[i])
        xf, xb = _merge(o1.reshape(s, WIDTH_A), l1.reshape(s, WIDTH_A), o4, l4, o16, l16, yb, g, xf,
                        b_gate3, w_br_a_b, w_br_b_b, w_out_b, lmg, lmb, i)
        j = i // 2
        if i % 2 == 0:
            xf, xb = _dense_ffn(xb, xf, w1d, w3d, w2d, lfg, lfb, i, j)
        else:
            xf, xb, xbuf = _moe_ffn(xf, wr_hi[j], wr_lo[j], w1e, w3e, w2e, lfg, lfb, i, j, xbuf_init=xbuf)
    return xf.reshape(b, s, d)
```
